```python
import math
import jax, jax.numpy as jnp
from jax import lax
import numpy as np

D_MODEL = 2048
BATCH = 1
SEQ = 16384
DEPTH = 1

N_MEM = 256
D_MIX = D_MODEL
D_CONV = 3 * D_MIX // 8
D_POOL = 3 * D_MIX // 8
D_XATT = D_MIX // 4
CONV_HEAD_DIM = 128
CONV_HEADS = D_CONV // CONV_HEAD_DIM
CONV_WIDTH = 3
POOL_WINDOWS = (2, 4, 8, 16)
POOL_GROUPS = len(POOL_WINDOWS)
POOL_GROUP_DIM = D_POOL // POOL_GROUPS
XATT_HEADS = 4
XATT_HEAD_DIM = D_XATT // XATT_HEADS
D_FF = ((8 * D_MODEL // 3 + 255) // 256) * 256
FFN_CONV_WIDTH = 3
D_IN = 3 * D_CONV + D_POOL + D_XATT
EPS = 1e-6

kernel_name = "hybrid_conv_pool_memxattn_block"


def rmsnorm(x, g):
    xf = x.astype(jnp.float32)
    y = xf * lax.rsqrt(jnp.mean(xf * xf, axis=-1, keepdims=True) + EPS)
    return (y * g.astype(jnp.float32)).astype(x.dtype)


def causal_dwconv(u, w):
    k = w.shape[0]
    s = u.shape[1]
    up = jnp.pad(u, ((0, 0), (k - 1, 0), (0, 0)))
    out = w[k - 1] * u
    for j in range(k - 1):
        out = out + w[j] * up[:, j:j + s]
    return out


def causal_pool_minus_self(v):
    s = v.shape[1]
    vf = v.astype(jnp.float32)
    c = jnp.pad(jnp.cumsum(vf, axis=1), ((0, 0), (1, 0), (0, 0), (0, 0)))
    t = jnp.arange(s, dtype=jnp.float32)[None, :, None]
    outs = []
    for g, k in enumerate(POOL_WINDOWS):
        cg = c[:, :, g]
        hi = cg[:, 1:]
        lo = jnp.pad(cg[:, :s + 1 - k], ((0, 0), (k - 1, 0), (0, 0)))
        cnt = jnp.minimum(t + 1.0, float(k))
        outs.append((hi - lo) / cnt - vf[:, :, g])
    return jnp.stack(outs, axis=2).astype(v.dtype)


def setup_inputs(seed: int = 0) -> dict:
    key = jax.random.key(seed)
    ks = jax.random.split(key, 18)
    f32 = jnp.float32
    nrm = lambda k, shape, scale: (jax.random.normal(k, shape, f32) * scale)
    gain = lambda k, shape: 1.0 + 0.05 * jax.random.normal(k, shape, f32)
    L = DEPTH
    return {
        "x": jax.random.normal(ks[0], (BATCH, SEQ, D_MODEL), f32),
        "mem": jax.random.normal(ks[1], (BATCH, N_MEM, D_MODEL), f32),
        "g_mix": gain(ks[2], (L, D_MODEL)),
        "g_mem": gain(ks[3], (L, D_MODEL)),
        "w_in": nrm(ks[4], (L, D_MODEL, D_IN), D_MODEL ** -0.5),
        "conv_w": nrm(ks[5], (L, CONV_WIDTH, D_CONV), CONV_WIDTH ** -0.5),
        "pool_w": nrm(ks[6], (L, POOL_GROUPS, POOL_GROUP_DIM, POOL_GROUP_DIM), POOL_GROUP_DIM ** -0.5),
        "pool_scale": gain(ks[7], (L, D_POOL)),
        "w_kv": nrm(ks[8], (L, D_MODEL, 2 * D_XATT), D_MODEL ** -0.5),
        "w_out": nrm(ks[9], (L, D_MIX, D_MODEL), D_MIX ** -0.5),
        "g_ffn": gain(ks[10], (L, D_MODEL)),
        "w_up": nrm(ks[11], (L, D_MODEL, 2 * D_FF), D_MODEL ** -0.5),
        "ffn_conv_w": nrm(ks[12], (L, FFN_CONV_WIDTH, 2 * D_FF), FFN_CONV_WIDTH ** -0.5),
        "ffn_conv_b": nrm(ks[13], (L, 2 * D_FF), 0.01),
        "w_down": nrm(ks[14], (L, D_FF, D_MODEL), D_FF ** -0.5),
        "g_final": gain(ks[15], (D_MODEL,)),
    }


def reference(x, mem, g_mix, g_mem, w_in, conv_w, pool_w, pool_scale, w_kv, w_out,
              g_ffn, w_up, ffn_conv_w, ffn_conv_b, w_down, g_final):
    b, s, _ = x.shape
    split_pts = [D_CONV, 2 * D_CONV, 3 * D_CONV, 3 * D_CONV + D_POOL]
    att_scale = 1.0 / math.sqrt(XATT_HEAD_DIM)
    for l in range(DEPTH):
        h = rmsnorm(x, g_mix[l])
        z = h @ w_in[l]
        cb, cc, cx, pv, q = jnp.split(z, split_pts, axis=-1)

        conv_out = cb * causal_dwconv(cc * cx, conv_w[l])

        pooled = causal_pool_minus_self(pv.reshape(b, s, POOL_GROUPS, POOL_GROUP_DIM))
        pool_out = jnp.einsum('bsgc,gcd->bsgd', pooled, pool_w[l]).reshape(b, s, D_POOL)
        pool_out = pool_out * pool_scale[l]

        m = rmsnorm(mem, g_mem[l])
        kv = m @ w_kv[l]
        k, v = jnp.split(kv, 2, axis=-1)
        k = k.reshape(b, N_MEM, XATT_HEADS, XATT_HEAD_DIM)
        v = v.reshape(b, N_MEM, XATT_HEADS, XATT_HEAD_DIM)
        qh = q.reshape(b, s, XATT_HEADS, XATT_HEAD_DIM)
        scores = jnp.einsum('bshd,bmhd->bhsm', qh, k).astype(jnp.float32) * att_scale
        probs = jax.nn.softmax(scores, axis=-1).astype(x.dtype)
        att = jnp.einsum('bhsm,bmhd->bshd', probs, v).reshape(b, s, D_XATT)

        mix = jnp.concatenate([conv_out, pool_out, att], axis=-1)
        x = x + mix @ w_out[l]

        h = rmsnorm(x, g_ffn[l])
        u = causal_dwconv(h @ w_up[l], ffn_conv_w[l]) + ffn_conv_b[l]
        gate, val = jnp.split(u, 2, axis=-1)
        x = x + (jax.nn.silu(gate) * val) @ w_down[l]
    return rmsnorm(x, g_final)
```

```python
import functools
import math

import jax
import jax.numpy as jnp
from jax import lax
from jax.experimental import pallas as pl
from jax.experimental.pallas import tpu as pltpu

EPS = 1e-6
POOL_WINDOWS = (2, 4, 8, 16)
XATT_HEADS = 4
CONV_WIDTH = 3

V7X_SUBLANES = 8
V7X_LANES = 128
V7X_VMEM_BYTES = 64 * 1024 * 1024

MIX_ROWS = 256
FFN_ROWS = 512
FFN_COLS = 512
POOL_HALO = 16
CONV_HALO = V7X_SUBLANES


def _rmsnorm(x, g):
    return x * lax.rsqrt(jnp.mean(x * x, axis=-1, keepdims=True) + EPS) * g


def _dot(a, b):
    return jnp.dot(a, b, preferred_element_type=jnp.float32)


def _causal_conv3(buf_ref, w_ref, rows):
    out = w_ref[2:3, :] * buf_ref[CONV_HALO:CONV_HALO + rows, :]
    for j in range(CONV_WIDTH - 1):
        lag = CONV_WIDTH - 1 - j
        out = out + w_ref[j:j + 1, :] * buf_ref[CONV_HALO - lag:CONV_HALO - lag + rows, :]
    return out


def _kv_kernel(mem_ref, g_ref, w_ref, k_ref, v_ref):
    m = _rmsnorm(mem_ref[...], g_ref[...]).astype(jnp.bfloat16)
    kv = _dot(m, w_ref[...])
    d = k_ref.shape[1]
    k_ref[...] = kv[:, :d].astype(jnp.bfloat16)
    v_ref[...] = kv[:, d:].astype(jnp.bfloat16)


def _kv_proj(mem, g_mem, w_kv):
    n_mem, _ = mem.shape
    d_xatt = w_kv.shape[1] // 2
    out = jax.ShapeDtypeStruct((n_mem, d_xatt), jnp.bfloat16)
    return pl.pallas_call(_kv_kernel, out_shape=(out, out), name="kv_proj")(mem, g_mem, w_kv)


def _mixer_kernel(x_ref, gmix_ref, win_ref, convw_ref, poolw_ref, pscale_ref, k_ref, v_ref,
                  wout_ref, gffn_ref, x1_ref, h2_ref, pbuf, vbuf, mixbuf, *, d_conv, d_pool, head_dim):
    i = pl.program_id(0)
    rows = x_ref.shape[0]

    @pl.when(i == 0)
    def _():
        pbuf[0:CONV_HALO, :] = jnp.zeros((CONV_HALO, d_conv), jnp.float32)
        vbuf[0:POOL_HALO, :] = jnp.zeros((POOL_HALO, d_pool), jnp.float32)

    x = x_ref[...]
    h = _rmsnorm(x, gmix_ref[...]).astype(jnp.bfloat16)

    c0 = d_conv
    cc = _dot(h, win_ref[:, c0:2 * c0])
    cx = _dot(h, win_ref[:, 2 * c0:3 * c0])
    pbuf[CONV_HALO:CONV_HALO + rows, :] = cc * cx
    conv = _causal_conv3(pbuf, convw_ref, rows)
    pbuf[0:CONV_HALO, :] = pbuf[rows:rows + CONV_HALO, :]
    cb = _dot(h, win_ref[:, 0:c0])
    mixbuf[:, 0:c0] = (cb * conv).astype(jnp.bfloat16)

    p0 = 3 * d_conv
    vbuf[POOL_HALO:POOL_HALO + rows, :] = _dot(h, win_ref[:, p0:p0 + d_pool])
    group_dim = d_pool // len(POOL_WINDOWS)
    t = (i * rows + lax.broadcasted_iota(jnp.int32, (rows, 1), 0)).astype(jnp.float32)
    for c in range(0, d_pool, V7X_LANES):
        col = c + lax.broadcasted_iota(jnp.int32, (1, V7X_LANES), 1)
        cur = vbuf[POOL_HALO:POOL_HALO + rows, c:c + V7X_LANES]
        acc = cur
        lag = 1
        pooled = None
        for g in range(c // group_dim, (c + V7X_LANES - 1) // group_dim + 1):
            k = POOL_WINDOWS[g]
            while lag < k:
                acc = acc + vbuf[POOL_HALO - lag:POOL_HALO - lag + rows, c:c + V7X_LANES]
                lag += 1
            mean_k = acc / jnp.minimum(t + 1.0, float(k))
            pooled = mean_k if pooled is None else jnp.where(col >= g * group_dim, mean_k, pooled)
        mixbuf[:, c0 + c:c0 + c + V7X_LANES] = (pooled - cur).astype(jnp.bfloat16)
    vbuf[0:POOL_HALO, :] = vbuf[rows:rows + POOL_HALO, :]
    pool_out = _dot(mixbuf[:, c0:c0 + d_pool], poolw_ref[...]) * pscale_ref[...]
    mixbuf[:, c0:c0 + d_pool] = pool_out.astype(jnp.bfloat16)

    q0 = p0 + d_pool
    a0 = c0 + d_pool
    att_scale = 1.0 / math.sqrt(head_dim)
    for hd in range(XATT_HEADS):
        lo = hd * head_dim
        q = _dot(h, win_ref[:, q0 + lo:q0 + lo + head_dim]).astype(jnp.bfloat16)
        s = lax.dot_general(q, k_ref[:, lo:lo + head_dim], (((1,), (1,)), ((), ())),
                            preferred_element_type=jnp.float32) * att_scale
        e = jnp.exp(s - jnp.max(s, axis=-1, keepdims=True))
        probs = (e / jnp.sum(e, axis=-1, keepdims=True)).astype(jnp.bfloat16)
        mixbuf[:, a0 + lo:a0 + lo + head_dim] = _dot(probs, v_ref[:, lo:lo + head_dim]).astype(jnp.bfloat16)

    x1 = x + _dot(mixbuf[...], wout_ref[...])
    x1_ref[...] = x1
    h2_ref[...] = _rmsnorm(x1, gffn_ref[...]).astype(jnp.bfloat16)


def _const_spec(shape):
    return pl.BlockSpec(shape, lambda i: (0,) * len(shape), pipeline_mode=pl.Buffered(1))


def _mixer(x, g_mix, w_in, conv_w, pool_bd, pool_scale, k, v, w_out, g_ffn, *, d_conv, d_pool):
    seq, d_model = x.shape
    rows = MIX_ROWS
    assert seq % rows == 0 and rows % POOL_HALO == 0
    head_dim = k.shape[1] // XATT_HEADS
    row_spec = pl.BlockSpec((rows, d_model), lambda i: (i, 0))
    consts = (g_mix, w_in, conv_w, pool_bd, pool_scale, k, v, w_out, g_ffn)
    return pl.pallas_call(
        functools.partial(_mixer_kernel, d_conv=d_conv, d_pool=d_pool, head_dim=head_dim),
        grid=(seq // rows,),
        in_specs=[row_spec] + [_const_spec(c.shape) for c in consts],
        out_specs=(row_spec, row_spec),
        out_shape=(jax.ShapeDtypeStruct((seq, d_model), jnp.float32),
                   jax.ShapeDtypeStruct((seq, d_model), jnp.bfloat16)),
        scratch_shapes=[
            pltpu.VMEM((rows + CONV_HALO, d_conv), jnp.float32),
            pltpu.VMEM((rows + POOL_HALO, d_pool), jnp.float32),
            pltpu.VMEM((rows, d_model), jnp.bfloat16),
        ],
        compiler_params=pltpu.CompilerParams(
            dimension_semantics=("arbitrary",),
            vmem_limit_bytes=_MIXER_VMEM_BYTES,
        ),
        name="token_mixer",
    )(x, *consts)


def _ffn_kernel(h2_ref, x1_ref, wg_ref, wv_ref, cwg_ref, cwv_ref, bg_ref, bv_ref, wd_ref, gfin_ref,
                o_ref, gbuf, vbuf, gcarry, vcarry):
    i = pl.program_id(0)
    j = pl.program_id(1)
    rows = h2_ref.shape[0]

    @pl.when(i == 0)
    def _():
        gcarry[j] = jnp.zeros(gcarry.shape[1:], jnp.float32)
        vcarry[j] = jnp.zeros(vcarry.shape[1:], jnp.float32)

    @pl.when(j == 0)
    def _():
        o_ref[...] = x1_ref[...]

    h2 = h2_ref[...]
    halves = []
    for w_ref, cw_ref, b_ref, buf, carry in ((wg_ref, cwg_ref, bg_ref, gbuf, gcarry),
                                             (wv_ref, cwv_ref, bv_ref, vbuf, vcarry)):
        buf[0:CONV_HALO, :] = carry[j]
        buf[CONV_HALO:CONV_HALO + rows, :] = _dot(h2, w_ref[...])
        carry[j] = buf[rows:rows + CONV_HALO, :]
        halves.append(_causal_conv3(buf, cw_ref, rows) + b_ref[...])
    gate, val = halves
    act = (gate * jax.nn.sigmoid(gate) * val).astype(jnp.bfloat16)
    o_ref[...] += _dot(act, wd_ref[...])

    @pl.when(j == pl.num_programs(1) - 1)
    def _():
        o_ref[...] = _rmsnorm(o_ref[...], gfin_ref[...])


def _ffn(h2, x1, w_up, ffn_conv_w, ffn_conv_b, w_down, g_final):
    seq, d_model = x1.shape
    d_ff = w_down.shape[0]
    rows, cols = FFN_ROWS, FFN_COLS
    assert seq % rows == 0 and d_ff % cols == 0
    n_j = d_ff // cols
    row_spec = lambda: pl.BlockSpec((rows, d_model), lambda i, j: (i, 0))
    gate_cols = lambda r: pl.BlockSpec((r, cols), lambda i, j: (0, j))
    val_cols = lambda r: pl.BlockSpec((r, cols), lambda i, j: (0, j + n_j))
    return pl.pallas_call(
        _ffn_kernel,
        grid=(seq // rows, n_j),
        in_specs=[
            row_spec(), row_spec(),
            gate_cols(d_model), val_cols(d_model),
            gate_cols(CONV_WIDTH), val_cols(CONV_WIDTH),
            gate_cols(1), val_cols(1),
            pl.BlockSpec((cols, d_model), lambda i, j: (j, 0)),
            pl.BlockSpec((1, d_model), lambda i, j: (0, 0)),
        ],
        out_specs=row_spec(),
        out_shape=jax.ShapeDtypeStruct((seq, d_model), jnp.float32),
        scratch_shapes=[
            pltpu.VMEM((rows + CONV_HALO, cols), jnp.float32),
            pltpu.VMEM((rows + CONV_HALO, cols), jnp.float32),
            pltpu.VMEM((n_j, CONV_HALO, cols), jnp.float32),
            pltpu.VMEM((n_j, CONV_HALO, cols), jnp.float32),
        ],
        compiler_params=pltpu.CompilerParams(
            dimension_semantics=("arbitrary", "arbitrary"),
            vmem_limit_bytes=_FFN_VMEM_BYTES,
        ),
        name="ffn",
    )(h2, x1, w_up, w_up, ffn_conv_w, ffn_conv_w, ffn_conv_b, ffn_conv_b, w_down, g_final)


_MIXER_VMEM_BYTES = 56 * 1024 * 1024
_FFN_VMEM_BYTES = 48 * 1024 * 1024


def _block_diag(pool_w):
    groups, dim, _ = pool_w.shape
    out = jnp.zeros((groups * dim, groups * dim), pool_w.dtype)
    for g in range(groups):
        out = out.at[g * dim:(g + 1) * dim, g * dim:(g + 1) * dim].set(pool_w[g])
    return out


def kernel(x, mem, g_mix, g_mem, w_in, conv_w, pool_w, pool_scale, w_kv, w_out, g_ffn, w_up,
           ffn_conv_w, ffn_conv_b, w_down, g_final):
    assert x.shape[0] == 1 and mem.shape[0] == 1 and w_in.shape[0] == 1
    assert tuple(sorted(POOL_WINDOWS)) == POOL_WINDOWS and pool_w.shape[1] == len(POOL_WINDOWS)
    bf16 = jnp.bfloat16
    d_conv = conv_w.shape[2]
    d_pool = pool_scale.shape[1]
    k, v = _kv_proj(mem[0], g_mem[0][None, :], w_kv[0].astype(bf16))
    x1, h2 = _mixer(x[0], g_mix[0][None, :], w_in[0].astype(bf16), conv_w[0],
                    _block_diag(pool_w[0]).astype(bf16), pool_scale[0][None, :], k, v,
                    w_out[0].astype(bf16), g_ffn[0][None, :], d_conv=d_conv, d_pool=d_pool)
    out = _ffn(h2, x1, w_up[0].astype(bf16), ffn_conv_w[0], ffn_conv_b[0][None, :],
               w_down[0].astype(bf16), g_final[None, :])
    return out[None]
```

```python
import functools
import math

import jax
import jax.numpy as jnp
from jax import lax
from jax.experimental import pallas as pl
from jax.experimental.pallas import tpu as pltpu

EPS = 1e-6
POOL_WINDOWS = (2, 4, 8, 16)
XATT_HEADS = 4
CONV_WIDTH = 3

V7X_SUBLANES = 8
V7X_LANES = 128
V7X_VMEM_BYTES = 64 * 1024 * 1024

MIX_ROWS = 256
FFN_ROWS = 512
FFN_COLS = 512
POOL_HALO = 16
CONV_HALO = V7X_SUBLANES


def _rmsnorm(x, g):
    return x * lax.rsqrt(jnp.mean(x * x, axis=-1, keepdims=True) + EPS) * g


def _dot(a, b):
    return jnp.dot(a, b, preferred_element_type=jnp.float32)


def _causal_conv3(buf_ref, w_ref, rows):
    out = w_ref[2:3, :] * buf_ref[CONV_HALO:CONV_HALO + rows, :]
    for j in range(CONV_WIDTH - 1):
        lag = CONV_WIDTH - 1 - j
        out = out + w_ref[j:j + 1, :] * buf_ref[CONV_HALO - lag:CONV_HALO - lag + rows, :]
    return out


def _kv_kernel(mem_ref, g_ref, w_ref, k_ref, v_ref):
    m = _rmsnorm(mem_ref[...], g_ref[...]).astype(jnp.bfloat16)
    kv = _dot(m, w_ref[...])
    heads, _, head_dim = k_ref.shape
    for hd in range(heads):
        k_ref[hd] = kv[:, hd * head_dim:(hd + 1) * head_dim].astype(jnp.bfloat16)
        v_ref[hd] = kv[:, (heads + hd) * head_dim:(heads + hd + 1) * head_dim].astype(jnp.bfloat16)


def _kv_proj(mem, g_mem, w_kv):
    n_mem, _ = mem.shape
    head_dim = w_kv.shape[1] // (2 * XATT_HEADS)
    out = jax.ShapeDtypeStruct((XATT_HEADS, n_mem, head_dim), jnp.bfloat16)
    return pl.pallas_call(_kv_kernel, out_shape=(out, out), name="kv_proj")(mem, g_mem, w_kv)


def _mixer_kernel(x_ref, gmix_ref, win_ref, convw_ref, poolw_ref, pscale_ref, k_ref, v_ref,
                  wout_ref, gffn_ref, x1_ref, h2_ref, zbuf, pbuf, vbuf, mixbuf, *, d_conv, d_pool, head_dim):
    i = pl.program_id(0)
    rows = x_ref.shape[0]

    @pl.when(i == 0)
    def _():
        pbuf[0:CONV_HALO, :] = jnp.zeros((CONV_HALO, d_conv), jnp.float32)
        vbuf[0:POOL_HALO, :] = jnp.zeros((POOL_HALO, d_pool), jnp.float32)

    x = x_ref[...]
    h = _rmsnorm(x, gmix_ref[...]).astype(jnp.bfloat16)
    zbuf[...] = _dot(h, win_ref[...])

    c0 = d_conv
    pbuf[CONV_HALO:CONV_HALO + rows, :] = zbuf[:, c0:2 * c0] * zbuf[:, 2 * c0:3 * c0]
    conv = _causal_conv3(pbuf, convw_ref, rows)
    pbuf[0:CONV_HALO, :] = pbuf[rows:rows + CONV_HALO, :]
    mixbuf[:, 0:c0] = (zbuf[:, 0:c0] * conv).astype(jnp.bfloat16)

    p0 = 3 * d_conv
    vbuf[POOL_HALO:POOL_HALO + rows, :] = zbuf[:, p0:p0 + d_pool]
    group_dim = d_pool // len(POOL_WINDOWS)
    t = (i * rows + lax.broadcasted_iota(jnp.int32, (rows, 1), 0)).astype(jnp.float32)
    for c in range(0, d_pool, V7X_LANES):
        col = c + lax.broadcasted_iota(jnp.int32, (1, V7X_LANES), 1)
        cur = vbuf[POOL_HALO:POOL_HALO + rows, c:c + V7X_LANES]
        acc = cur
        lag = 1
        pooled = None
        for g in range(c // group_dim, (c + V7X_LANES - 1) // group_dim + 1):
            k = POOL_WINDOWS[g]
            while lag < k:
                acc = acc + vbuf[POOL_HALO - lag:POOL_HALO - lag + rows, c:c + V7X_LANES]
                lag += 1
            mean_k = acc / jnp.minimum(t + 1.0, float(k))
            pooled = mean_k if pooled is None else jnp.where(col >= g * group_dim, mean_k, pooled)
        mixbuf[:, c0 + c:c0 + c + V7X_LANES] = (pooled - cur).astype(jnp.bfloat16)
    vbuf[0:POOL_HALO, :] = vbuf[rows:rows + POOL_HALO, :]
    pool_out = _dot(mixbuf[:, c0:c0 + d_pool], poolw_ref[...]) * pscale_ref[...]
    mixbuf[:, c0:c0 + d_pool] = pool_out.astype(jnp.bfloat16)

    q0 = p0 + d_pool
    a0 = c0 + d_pool
    q = jnp.stack([zbuf[:, q0 + hd * head_dim:q0 + (hd + 1) * head_dim] for hd in range(XATT_HEADS)])
    s = jnp.einsum("hqd,hmd->hqm", q.astype(jnp.bfloat16), k_ref[...],
                   preferred_element_type=jnp.float32) * (1.0 / math.sqrt(head_dim))
    e = jnp.exp(s - jnp.max(s, axis=-1, keepdims=True))
    probs = (e / jnp.sum(e, axis=-1, keepdims=True)).astype(jnp.bfloat16)
    att = jnp.einsum("hqm,hmd->hqd", probs, v_ref[...], preferred_element_type=jnp.float32)
    for hd in range(XATT_HEADS):
        mixbuf[:, a0 + hd * head_dim:a0 + (hd + 1) * head_dim] = att[hd].astype(jnp.bfloat16)

    x1 = x + _dot(mixbuf[...], wout_ref[...])
    x1_ref[...] = x1
    h2_ref[...] = _rmsnorm(x1, gffn_ref[...]).astype(jnp.bfloat16)


def _const_spec(shape):
    return pl.BlockSpec(shape, lambda i: (0,) * len(shape), pipeline_mode=pl.Buffered(1))


def _mixer(x, g_mix, w_in, conv_w, pool_bd, pool_scale, k, v, w_out, g_ffn, *, d_conv, d_pool):
    seq, d_model = x.shape
    rows = MIX_ROWS
    assert seq % rows == 0 and rows % POOL_HALO == 0
    head_dim = k.shape[2]
    row_spec = pl.BlockSpec((rows, d_model), lambda i: (i, 0))
    consts = (g_mix, w_in, conv_w, pool_bd, pool_scale, k, v, w_out, g_ffn)
    return pl.pallas_call(
        functools.partial(_mixer_kernel, d_conv=d_conv, d_pool=d_pool, head_dim=head_dim),
        grid=(seq // rows,),
        in_specs=[row_spec] + [_const_spec(c.shape) for c in consts],
        out_specs=(row_spec, row_spec),
        out_shape=(jax.ShapeDtypeStruct((seq, d_model), jnp.float32),
                   jax.ShapeDtypeStruct((seq, d_model), jnp.bfloat16)),
        scratch_shapes=[
            pltpu.VMEM((rows, w_in.shape[1]), jnp.float32),
            pltpu.VMEM((rows + CONV_HALO, d_conv), jnp.float32),
            pltpu.VMEM((rows + POOL_HALO, d_pool), jnp.float32),
            pltpu.VMEM((rows, d_model), jnp.bfloat16),
        ],
        compiler_params=pltpu.CompilerParams(
            dimension_semantics=("arbitrary",),
            vmem_limit_bytes=_MIXER_VMEM_BYTES,
        ),
        name="token_mixer",
    )(x, *consts)


def _ffn_kernel(h2_ref, x1_ref, wg_ref, wv_ref, cwg_ref, cwv_ref, bg_ref, bv_ref, wd_ref, gfin_ref,
                o_ref, gbuf, vbuf, gcarry, vcarry):
    i = pl.program_id(0)
    j = pl.program_id(1)
    rows = h2_ref.shape[0]

    @pl.when(i == 0)
    def _():
        gcarry[j] = jnp.zeros(gcarry.shape[1:], jnp.float32)
        vcarry[j] = jnp.zeros(vcarry.shape[1:], jnp.float32)

    @pl.when(j == 0)
    def _():
        o_ref[...] = x1_ref[...]

    h2 = h2_ref[...]
    halves = []
    for w_ref, cw_ref, b_ref, buf, carry in ((wg_ref, cwg_ref, bg_ref, gbuf, gcarry),
                                             (wv_ref, cwv_ref, bv_ref, vbuf, vcarry)):
        buf[0:CONV_HALO, :] = carry[j]
        buf[CONV_HALO:CONV_HALO + rows, :] = _dot(h2, w_ref[...])
        carry[j] = buf[rows:rows + CONV_HALO, :]
        halves.append(_causal_conv3(buf, cw_ref, rows) + b_ref[...])
    gate, val = halves
    act = (gate * jax.nn.sigmoid(gate) * val).astype(jnp.bfloat16)
    o_ref[...] += _dot(act, wd_ref[...])

    @pl.when(j == pl.num_programs(1) - 1)
    def _():
        o_ref[...] = _rmsnorm(o_ref[...], gfin_ref[...])


def _ffn(h2, x1, w_up, ffn_conv_w, ffn_conv_b, w_down, g_final):
    seq, d_model = x1.shape
    d_ff = w_down.shape[0]
    rows, cols = FFN_ROWS, FFN_COLS
    assert seq % rows == 0 and d_ff % cols == 0
    n_j = d_ff // cols
    row_spec = lambda: pl.BlockSpec((rows, d_model), lambda i, j: (i, 0))
    gate_cols = lambda r: pl.BlockSpec((r, cols), lambda i, j: (0, j))
    val_cols = lambda r: pl.BlockSpec((r, cols), lambda i, j: (0, j + n_j))
    return pl.pallas_call(
        _ffn_kernel,
        grid=(seq // rows, n_j),
        in_specs=[
            row_spec(), row_spec(),
            gate_cols(d_model), val_cols(d_model),
            gate_cols(CONV_WIDTH), val_cols(CONV_WIDTH),
            gate_cols(1), val_cols(1),
            pl.BlockSpec((cols, d_model), lambda i, j: (j, 0)),
            pl.BlockSpec((1, d_model), lambda i, j: (0, 0)),
        ],
        out_specs=row_spec(),
        out_shape=jax.ShapeDtypeStruct((seq, d_model), jnp.float32),
        scratch_shapes=[
            pltpu.VMEM((rows + CONV_HALO, cols), jnp.float32),
            pltpu.VMEM((rows + CONV_HALO, cols), jnp.float32),
            pltpu.VMEM((n_j, CONV_HALO, cols), jnp.float32),
            pltpu.VMEM((n_j, CONV_HALO, cols), jnp.float32),
        ],
        compiler_params=pltpu.CompilerParams(
            dimension_semantics=("arbitrary", "arbitrary"),
            vmem_limit_bytes=_FFN_VMEM_BYTES,
        ),
        name="ffn",
    )(h2, x1, w_up, w_up, ffn_conv_w, ffn_conv_w, ffn_conv_b, ffn_conv_b, w_down, g_final)


_MIXER_VMEM_BYTES = 56 * 1024 * 1024
_FFN_VMEM_BYTES = 48 * 1024 * 1024


def _block_diag(pool_w):
    groups, dim, _ = pool_w.shape
    out = jnp.zeros((groups * dim, groups * dim), pool_w.dtype)
    for g in range(groups):
        out = out.at[g * dim:(g + 1) * dim, g * dim:(g + 1) * dim].set(pool_w[g])
    return out


def kernel(x, mem, g_mix, g_mem, w_in, conv_w, pool_w, pool_scale, w_kv, w_out, g_ffn, w_up,
           ffn_conv_w, ffn_conv_b, w_down, g_final):
    assert x.shape[0] == 1 and mem.shape[0] == 1 and w_in.shape[0] == 1
    assert tuple(sorted(POOL_WINDOWS)) == POOL_WINDOWS and pool_w.shape[1] == len(POOL_WINDOWS)
    bf16 = jnp.bfloat16
    d_conv = conv_w.shape[2]
    d_pool = pool_scale.shape[1]
    k, v = _kv_proj(mem[0], g_mem[0][None, :], w_kv[0].astype(bf16))
    x1, h2 = _mixer(x[0], g_mix[0][None, :], w_in[0].astype(bf16), conv_w[0],
                    _block_diag(pool_w[0]).astype(bf16), pool_scale[0][None, :], k, v,
                    w_out[0].astype(bf16), g_ffn[0][None, :], d_conv=d_conv, d_pool=d_pool)
    out = _ffn(h2, x1, w_up[0].astype(bf16), ffn_conv_w[0], ffn_conv_b[0][None, :],
               w_down[0].astype(bf16), g_final[None, :])
    return out[None]
```

```python
import functools
import math

import jax
import jax.numpy as jnp
from jax import lax
from jax.experimental import pallas as pl
from jax.experimental.pallas import tpu as pltpu

EPS = 1e-6
POOL_WINDOWS = (2, 4, 8, 16)
XATT_HEADS = 4
CONV_WIDTH = 3

V7X_SUBLANES = 8
V7X_LANES = 128
V7X_VMEM_BYTES = 64 * 1024 * 1024

MIX_ROWS = 256
FFN_ROWS = 1024
FFN_COLS = 512
FFN_SUB_ROWS = 512
FFN_OUT_COLS = 512
FFN_NORM_ROWS = 128
POOL_HALO = 16
CONV_HALO = V7X_SUBLANES


def _rmsnorm(x, g):
    return x * lax.rsqrt(jnp.mean(x * x, axis=-1, keepdims=True) + EPS) * g


def _dot(a, b):
    return jnp.dot(a, b, preferred_element_type=jnp.float32)


def _causal_conv3(buf_ref, w_ref, rows):
    out = w_ref[2:3, :] * buf_ref[CONV_HALO:CONV_HALO + rows, :]
    for j in range(CONV_WIDTH - 1):
        lag = CONV_WIDTH - 1 - j
        out = out + w_ref[j:j + 1, :] * buf_ref[CONV_HALO - lag:CONV_HALO - lag + rows, :]
    return out


def _kv_kernel(mem_ref, g_ref, w_ref, k_ref, v_ref):
    m = _rmsnorm(mem_ref[...], g_ref[...]).astype(jnp.bfloat16)
    kv = _dot(m, w_ref[...])
    heads, _, head_dim = k_ref.shape
    for hd in range(heads):
        k_ref[hd] = kv[:, hd * head_dim:(hd + 1) * head_dim].astype(jnp.bfloat16)
        v_ref[hd] = kv[:, (heads + hd) * head_dim:(heads + hd + 1) * head_dim].astype(jnp.bfloat16)


def _kv_proj(mem, g_mem, w_kv):
    n_mem, _ = mem.shape
    head_dim = w_kv.shape[1] // (2 * XATT_HEADS)
    out = jax.ShapeDtypeStruct((XATT_HEADS, n_mem, head_dim), jnp.bfloat16)
    return pl.pallas_call(_kv_kernel, out_shape=(out, out), name="kv_proj")(mem, g_mem, w_kv)


def _mixer_kernel(x_ref, gmix_ref, win_ref, convw_ref, poolw_ref, pscale_ref, k_ref, v_ref,
                  wout_ref, gffn_ref, x1_ref, h2_ref, zbuf, pbuf, vbuf, mixbuf, *, d_conv, d_pool, head_dim):
    i = pl.program_id(0)
    rows = x_ref.shape[0]

    @pl.when(i == 0)
    def _():
        pbuf[0:CONV_HALO, :] = jnp.zeros((CONV_HALO, d_conv), jnp.float32)
        vbuf[0:POOL_HALO, :] = jnp.zeros((POOL_HALO, d_pool), jnp.float32)

    x = x_ref[...]
    h = _rmsnorm(x, gmix_ref[...]).astype(jnp.bfloat16)
    zbuf[...] = _dot(h, win_ref[...])

    c0 = d_conv
    pbuf[CONV_HALO:CONV_HALO + rows, :] = zbuf[:, c0:2 * c0] * zbuf[:, 2 * c0:3 * c0]
    conv = _causal_conv3(pbuf, convw_ref, rows)
    pbuf[0:CONV_HALO, :] = pbuf[rows:rows + CONV_HALO, :]
    mixbuf[:, 0:c0] = (zbuf[:, 0:c0] * conv).astype(jnp.bfloat16)

    p0 = 3 * d_conv
    vbuf[POOL_HALO:POOL_HALO + rows, :] = zbuf[:, p0:p0 + d_pool]
    group_dim = d_pool // len(POOL_WINDOWS)
    t = (i * rows + lax.broadcasted_iota(jnp.int32, (rows, 1), 0)).astype(jnp.float32)
    for c in range(0, d_pool, V7X_LANES):
        col = c + lax.broadcasted_iota(jnp.int32, (1, V7X_LANES), 1)
        cur = vbuf[POOL_HALO:POOL_HALO + rows, c:c + V7X_LANES]
        acc = cur
        lag = 1
        pooled = None
        for g in range(c // group_dim, (c + V7X_LANES - 1) // group_dim + 1):
            k = POOL_WINDOWS[g]
            while lag < k:
                acc = acc + vbuf[POOL_HALO - lag:POOL_HALO - lag + rows, c:c + V7X_LANES]
                lag += 1
            mean_k = acc / jnp.minimum(t + 1.0, float(k))
            pooled = mean_k if pooled is None else jnp.where(col >= g * group_dim, mean_k, pooled)
        mixbuf[:, c0 + c:c0 + c + V7X_LANES] = (pooled - cur).astype(jnp.bfloat16)
    vbuf[0:POOL_HALO, :] = vbuf[rows:rows + POOL_HALO, :]
    pool_out = _dot(mixbuf[:, c0:c0 + d_pool], poolw_ref[...]) * pscale_ref[...]
    mixbuf[:, c0:c0 + d_pool] = pool_out.astype(jnp.bfloat16)

    q0 = p0 + d_pool
    a0 = c0 + d_pool
    q = jnp.stack([zbuf[:, q0 + hd * head_dim:q0 + (hd + 1) * head_dim] for hd in range(XATT_HEADS)])
    s = jnp.einsum("hqd,hmd->hqm", q.astype(jnp.bfloat16), k_ref[...],
                   preferred_element_type=jnp.float32) * (1.0 / math.sqrt(head_dim))
    e = jnp.exp(s - jnp.max(s, axis=-1, keepdims=True))
    probs = (e / jnp.sum(e, axis=-1, keepdims=True)).astype(jnp.bfloat16)
    att = jnp.einsum("hqm,hmd->hqd", probs, v_ref[...], preferred_element_type=jnp.float32)
    for hd in range(XATT_HEADS):
        mixbuf[:, a0 + hd * head_dim:a0 + (hd + 1) * head_dim] = att[hd].astype(jnp.bfloat16)

    x1 = x + _dot(mixbuf[...], wout_ref[...])
    x1_ref[...] = x1
    h2_ref[...] = _rmsnorm(x1, gffn_ref[...]).astype(jnp.bfloat16)


def _const_spec(shape):
    return pl.BlockSpec(shape, lambda i: (0,) * len(shape), pipeline_mode=pl.Buffered(1))


def _mixer(x, g_mix, w_in, conv_w, pool_bd, pool_scale, k, v, w_out, g_ffn, *, d_conv, d_pool):
    seq, d_model = x.shape
    rows = MIX_ROWS
    assert seq % rows == 0 and rows % POOL_HALO == 0
    head_dim = k.shape[2]
    row_spec = pl.BlockSpec((rows, d_model), lambda i: (i, 0))
    consts = (g_mix, w_in, conv_w, pool_bd, pool_scale, k, v, w_out, g_ffn)
    return pl.pallas_call(
        functools.partial(_mixer_kernel, d_conv=d_conv, d_pool=d_pool, head_dim=head_dim),
        grid=(seq // rows,),
        in_specs=[row_spec] + [_const_spec(c.shape) for c in consts],
        out_specs=(row_spec, row_spec),
        out_shape=(jax.ShapeDtypeStruct((seq, d_model), jnp.float32),
                   jax.ShapeDtypeStruct((seq, d_model), jnp.bfloat16)),
        scratch_shapes=[
            pltpu.VMEM((rows, w_in.shape[1]), jnp.float32),
            pltpu.VMEM((rows + CONV_HALO, d_conv), jnp.float32),
            pltpu.VMEM((rows + POOL_HALO, d_pool), jnp.float32),
            pltpu.VMEM((rows, d_model), jnp.bfloat16),
        ],
        compiler_params=pltpu.CompilerParams(
            dimension_semantics=("arbitrary",),
            vmem_limit_bytes=_MIXER_VMEM_BYTES,
        ),
        name="token_mixer",
    )(x, *consts)


def _ffn_kernel(h2_ref, x1_hbm, wg_ref, wv_ref, cwg_ref, cwv_ref, bg_ref, bv_ref, wd_ref, gfin_ref,
                o_ref, gbuf, vbuf, gcarry, vcarry, actbuf, x1buf, x1_sem, *, n_j):
    t = pl.program_id(0)
    n_chunks = pl.num_programs(0) - 1
    rows = h2_ref.shape[0]
    up = jnp.minimum(t, n_chunks - 1)
    down = jnp.maximum(t - 1, 0)
    j_up, i_up = up % n_j, up // n_j
    j_down, i_down = down % n_j, down // n_j
    real_down = t >= 1

    def x1_copy():
        return pltpu.make_async_copy(x1_hbm.at[pl.ds(i_down * rows, rows), :], x1buf, x1_sem)

    @pl.when(t == 0)
    def _():
        actbuf[1] = jnp.zeros(actbuf.shape[1:], actbuf.dtype)

    @pl.when(i_up == 0)
    def _():
        gcarry[j_up] = jnp.zeros(gcarry.shape[1:], jnp.float32)
        vcarry[j_up] = jnp.zeros(vcarry.shape[1:], jnp.float32)

    @pl.when(j_down == 0)
    def _():
        o_ref[...] = jnp.zeros(o_ref.shape, jnp.float32)

    @pl.when(real_down & (j_down == 0))
    def _():
        x1_copy().start()

    @pl.when(real_down & (j_down == n_j - 1))
    def _():
        x1_copy().wait()

    gbuf[0:CONV_HALO, :] = gcarry[j_up]
    vbuf[0:CONV_HALO, :] = vcarry[j_up]
    for r in range(0, rows, FFN_SUB_ROWS):
        h2 = h2_ref[r:r + FFN_SUB_ROWS, :]
        gbuf[CONV_HALO + r:CONV_HALO + r + FFN_SUB_ROWS, :] = _dot(h2, wg_ref[...])
        vbuf[CONV_HALO + r:CONV_HALO + r + FFN_SUB_ROWS, :] = _dot(h2, wv_ref[...])
        gate = _causal_conv3(gbuf.at[r:r + CONV_HALO + FFN_SUB_ROWS, :], cwg_ref, FFN_SUB_ROWS) + bg_ref[...]
        val = _causal_conv3(vbuf.at[r:r + CONV_HALO + FFN_SUB_ROWS, :], cwv_ref, FFN_SUB_ROWS) + bv_ref[...]
        actbuf[t % 2, r:r + FFN_SUB_ROWS, :] = (gate * jax.nn.sigmoid(gate) * val).astype(jnp.bfloat16)
    gcarry[j_up] = gbuf[rows:rows + CONV_HALO, :]
    vcarry[j_up] = vbuf[rows:rows + CONV_HALO, :]

    act_prev = actbuf[(t + 1) % 2]
    for n in range(0, o_ref.shape[1], FFN_OUT_COLS):
        o_ref[:, n:n + FFN_OUT_COLS] += _dot(act_prev, wd_ref[:, n:n + FFN_OUT_COLS])

    @pl.when(real_down & (j_down == n_j - 1))
    def _():
        def norm_rows(c, carry):
            r = pl.multiple_of(c * FFN_NORM_ROWS, FFN_NORM_ROWS)
            y = x1buf[pl.ds(r, FFN_NORM_ROWS), :] + o_ref[pl.ds(r, FFN_NORM_ROWS), :]
            o_ref[pl.ds(r, FFN_NORM_ROWS), :] = _rmsnorm(y, gfin_ref[...])
            return carry

        lax.fori_loop(0, rows // FFN_NORM_ROWS, norm_rows, None)


def _ffn(h2, x1, w_up, ffn_conv_w, ffn_conv_b, w_down, g_final):
    seq, d_model = x1.shape
    d_ff = w_down.shape[0]
    rows, cols = FFN_ROWS, FFN_COLS
    assert seq % rows == 0 and d_ff % cols == 0 and rows % FFN_NORM_ROWS == 0
    n_j = d_ff // cols
    n_chunks = (seq // rows) * n_j
    up = lambda t: jnp.minimum(t, n_chunks - 1)
    down = lambda t: jnp.maximum(t - 1, 0)
    gate_cols = lambda r: pl.BlockSpec((r, cols), lambda t: (0, up(t) % n_j))
    val_cols = lambda r: pl.BlockSpec((r, cols), lambda t: (0, up(t) % n_j + n_j))
    return pl.pallas_call(
        functools.partial(_ffn_kernel, n_j=n_j),
        grid=(n_chunks + 1,),
        in_specs=[
            pl.BlockSpec((rows, d_model), lambda t: (up(t) // n_j, 0)),
            pl.BlockSpec(memory_space=pl.ANY),
            gate_cols(d_model), val_cols(d_model),
            gate_cols(CONV_WIDTH), val_cols(CONV_WIDTH),
            gate_cols(1), val_cols(1),
            pl.BlockSpec((cols, d_model), lambda t: (down(t) % n_j, 0)),
            pl.BlockSpec((1, d_model), lambda t: (0, 0)),
        ],
        out_specs=pl.BlockSpec((rows, d_model), lambda t: (down(t) // n_j, 0)),
        out_shape=jax.ShapeDtypeStruct((seq, d_model), jnp.float32),
        scratch_shapes=[
            pltpu.VMEM((rows + CONV_HALO, cols), jnp.float32),
            pltpu.VMEM((rows + CONV_HALO, cols), jnp.float32),
            pltpu.VMEM((n_j, CONV_HALO, cols), jnp.float32),
            pltpu.VMEM((n_j, CONV_HALO, cols), jnp.float32),
            pltpu.VMEM((2, rows, cols), jnp.bfloat16),
            pltpu.VMEM((rows, d_model), jnp.float32),
            pltpu.SemaphoreType.DMA(()),
        ],
        compiler_params=pltpu.CompilerParams(
            dimension_semantics=("arbitrary",),
            vmem_limit_bytes=_FFN_VMEM_BYTES,
        ),
        name="ffn",
    )(h2, x1, w_up, w_up, ffn_conv_w, ffn_conv_w, ffn_conv_b, ffn_conv_b, w_down, g_final)


_MIXER_VMEM_BYTES = 56 * 1024 * 1024
_FFN_VMEM_BYTES = 64 * 1024 * 1024


def _block_diag(pool_w):
    groups, dim, _ = pool_w.shape
    out = jnp.zeros((groups * dim, groups * dim), pool_w.dtype)
    for g in range(groups):
        out = out.at[g * dim:(g + 1) * dim, g * dim:(g + 1) * dim].set(pool_w[g])
    return out


def kernel(x, mem, g_mix, g_mem, w_in, conv_w, pool_w, pool_scale, w_kv, w_out, g_ffn, w_up,
           ffn_conv_w, ffn_conv_b, w_down, g_final):
    assert x.shape[0] == 1 and mem.shape[0] == 1 and w_in.shape[0] == 1
    assert tuple(sorted(POOL_WINDOWS)) == POOL_WINDOWS and pool_w.shape[1] == len(POOL_WINDOWS)
    bf16 = jnp.bfloat16
    d_conv = conv_w.shape[2]
    d_pool = pool_scale.shape[1]
    k, v = _kv_proj(mem[0], g_mem[0][None, :], w_kv[0].astype(bf16))
    x1, h2 = _mixer(x[0], g_mix[0][None, :], w_in[0].astype(bf16), conv_w[0],
                    _block_diag(pool_w[0]).astype(bf16), pool_scale[0][None, :], k, v,
                    w_out[0].astype(bf16), g_ffn[0][None, :], d_conv=d_conv, d_pool=d_pool)
    out = _ffn(h2, x1, w_up[0].astype(bf16), ffn_conv_w[0], ffn_conv_b[0][None, :],
               w_down[0].astype(bf16), g_final[None, :])
    return out[None]
```

```python
import functools
import math

import jax
import jax.numpy as jnp
from jax import lax
from jax.experimental import pallas as pl
from jax.experimental.pallas import tpu as pltpu

EPS = 1e-6
POOL_WINDOWS = (2, 4, 8, 16)
XATT_HEADS = 4
CONV_WIDTH = 3

V7X_SUBLANES = 8
V7X_LANES = 128
BF16_SUBLANES = 2 * V7X_SUBLANES
V7X_VMEM_BYTES = 64 * 1024 * 1024

MIX_ROWS = 256
FFN_ROWS = 512
FFN_COLS = 512
POOL_HALO = 16
CONV_HALO = V7X_SUBLANES


def _rmsnorm(x, g):
    return x * lax.rsqrt(jnp.mean(x * x, axis=-1, keepdims=True) + EPS) * g


def _dot(a, b):
    return jnp.dot(a, b, preferred_element_type=jnp.float32)


def _causal_conv3(buf_ref, w_ref, rows):
    out = w_ref[2:3, :] * buf_ref[CONV_HALO:CONV_HALO + rows, :]
    for j in range(CONV_WIDTH - 1):
        lag = CONV_WIDTH - 1 - j
        out = out + w_ref[j:j + 1, :] * buf_ref[CONV_HALO - lag:CONV_HALO - lag + rows, :]
    return out


def _kv_kernel(mem_ref, g_ref, w_ref, k_ref, v_ref):
    m = _rmsnorm(mem_ref[...], g_ref[...]).astype(jnp.bfloat16)
    kv = _dot(m, w_ref[...])
    heads, _, head_dim = k_ref.shape
    for hd in range(heads):
        k_ref[hd] = kv[:, hd * head_dim:(hd + 1) * head_dim].astype(jnp.bfloat16)
        v_ref[hd] = kv[:, (heads + hd) * head_dim:(heads + hd + 1) * head_dim].astype(jnp.bfloat16)


def _kv_proj(mem, g_mem, w_kv):
    n_mem, _ = mem.shape
    head_dim = w_kv.shape[1] // (2 * XATT_HEADS)
    out = jax.ShapeDtypeStruct((XATT_HEADS, n_mem, head_dim), jnp.bfloat16)
    return pl.pallas_call(_kv_kernel, out_shape=(out, out), name="kv_proj")(mem, g_mem, w_kv)


def _mixer_kernel(x_ref, gmix_ref, win_ref, convw_ref, poolw_ref, pscale_ref, k_ref, v_ref,
                  wout_ref, gffn_ref, wup_ref, wdown_ref, x1_ref, h2_ref, wup16_ref, wdown16_ref,
                  zbuf, pbuf, vbuf, mixbuf, *, d_conv, d_pool, head_dim):
    i = pl.program_id(0)
    rows = x_ref.shape[0]

    ff_cols = wup16_ref.shape[2]
    for c in range(wup16_ref.shape[0]):
        wup16_ref[c] = wup_ref[:, c * ff_cols:(c + 1) * ff_cols].astype(jnp.bfloat16)
    wdown16_ref[...] = wdown_ref[...].astype(jnp.bfloat16)

    @pl.when(i == 0)
    def _():
        pbuf[0:CONV_HALO, :] = jnp.zeros((CONV_HALO, d_conv), jnp.float32)
        vbuf[0:POOL_HALO, :] = jnp.zeros((POOL_HALO, d_pool), jnp.float32)

    x = x_ref[...]
    h = _rmsnorm(x, gmix_ref[...]).astype(jnp.bfloat16)
    zbuf[...] = _dot(h, win_ref[...])

    c0 = d_conv
    pbuf[CONV_HALO:CONV_HALO + rows, :] = zbuf[:, c0:2 * c0] * zbuf[:, 2 * c0:3 * c0]
    conv = _causal_conv3(pbuf, convw_ref, rows)
    pbuf[0:CONV_HALO, :] = pbuf[rows:rows + CONV_HALO, :]
    mixbuf[:, 0:c0] = (zbuf[:, 0:c0] * conv).astype(jnp.bfloat16)

    p0 = 3 * d_conv
    vbuf[POOL_HALO:POOL_HALO + rows, :] = zbuf[:, p0:p0 + d_pool]
    group_dim = d_pool // len(POOL_WINDOWS)
    t = (i * rows + lax.broadcasted_iota(jnp.int32, (rows, 1), 0)).astype(jnp.float32)
    for c in range(0, d_pool, V7X_LANES):
        col = c + lax.broadcasted_iota(jnp.int32, (1, V7X_LANES), 1)
        cur = vbuf[POOL_HALO:POOL_HALO + rows, c:c + V7X_LANES]
        acc = cur
        lag = 1
        pooled = None
        for g in range(c // group_dim, (c + V7X_LANES - 1) // group_dim + 1):
            k = POOL_WINDOWS[g]
            while lag < k:
                acc = acc + vbuf[POOL_HALO - lag:POOL_HALO - lag + rows, c:c + V7X_LANES]
                lag += 1
            mean_k = acc / jnp.minimum(t + 1.0, float(k))
            pooled = mean_k if pooled is None else jnp.where(col >= g * group_dim, mean_k, pooled)
        mixbuf[:, c0 + c:c0 + c + V7X_LANES] = (pooled - cur).astype(jnp.bfloat16)
    vbuf[0:POOL_HALO, :] = vbuf[rows:rows + POOL_HALO, :]
    pool_out = _dot(mixbuf[:, c0:c0 + d_pool], poolw_ref[...]) * pscale_ref[...]
    mixbuf[:, c0:c0 + d_pool] = pool_out.astype(jnp.bfloat16)

    q0 = p0 + d_pool
    a0 = c0 + d_pool
    q = jnp.stack([zbuf[:, q0 + hd * head_dim:q0 + (hd + 1) * head_dim] for hd in range(XATT_HEADS)])
    s = jnp.einsum("hqd,hmd->hqm", q.astype(jnp.bfloat16), k_ref[...],
                   preferred_element_type=jnp.float32) * (1.0 / math.sqrt(head_dim))
    e = jnp.exp(s - jnp.max(s, axis=-1, keepdims=True))
    probs = (e / jnp.sum(e, axis=-1, keepdims=True)).astype(jnp.bfloat16)
    att = jnp.einsum("hqm,hmd->hqd", probs, v_ref[...], preferred_element_type=jnp.float32)
    for hd in range(XATT_HEADS):
        mixbuf[:, a0 + hd * head_dim:a0 + (hd + 1) * head_dim] = att[hd].astype(jnp.bfloat16)

    x1 = x + _dot(mixbuf[...], wout_ref[...])
    x1_ref[...] = x1
    h2_ref[...] = _rmsnorm(x1, gffn_ref[...]).astype(jnp.bfloat16)


def _const_spec(shape):
    return pl.BlockSpec(shape, lambda i: (0,) * len(shape), pipeline_mode=pl.Buffered(1))


def _mixer(x, g_mix, w_in, conv_w, pool_bd, pool_scale, k, v, w_out, g_ffn, w_up, w_down, *, d_conv, d_pool):
    seq, d_model = x.shape
    rows = MIX_ROWS
    steps = seq // rows
    assert seq % rows == 0 and rows % POOL_HALO == 0
    head_dim = k.shape[2]
    row_spec = pl.BlockSpec((rows, d_model), lambda i: (i, 0))
    consts = (g_mix, w_in, conv_w, pool_bd, pool_scale, k, v, w_out, g_ffn)
    d_ff = w_down.shape[0]
    up_chunks = w_up.shape[1] // FFN_COLS
    up_rows = d_model // steps
    down_rows = 2 * d_ff // steps
    assert d_model % steps == 0 and (2 * d_ff) % steps == 0 and steps % 2 == 0
    assert up_rows % BF16_SUBLANES == 0 and down_rows % BF16_SUBLANES == 0 and w_up.shape[1] % FFN_COLS == 0
    return pl.pallas_call(
        functools.partial(_mixer_kernel, d_conv=d_conv, d_pool=d_pool, head_dim=head_dim),
        grid=(steps,),
        in_specs=[row_spec] + [_const_spec(c.shape) for c in consts] + [
            pl.BlockSpec((up_rows, w_up.shape[1]), lambda i: (i, 0)),
            pl.BlockSpec((down_rows, d_model), lambda i: (i // 2, 0)),
        ],
        out_specs=(row_spec, row_spec,
                   pl.BlockSpec((up_chunks, up_rows, FFN_COLS), lambda i: (0, i, 0)),
                   pl.BlockSpec((down_rows, d_model), lambda i: (i // 2, 0))),
        out_shape=(jax.ShapeDtypeStruct((seq, d_model), jnp.float32),
                   jax.ShapeDtypeStruct((seq, d_model), jnp.bfloat16),
                   jax.ShapeDtypeStruct((up_chunks, d_model, FFN_COLS), jnp.bfloat16),
                   jax.ShapeDtypeStruct((d_ff, d_model), jnp.bfloat16)),
        scratch_shapes=[
            pltpu.VMEM((rows, w_in.shape[1]), jnp.float32),
            pltpu.VMEM((rows + CONV_HALO, d_conv), jnp.float32),
            pltpu.VMEM((rows + POOL_HALO, d_pool), jnp.float32),
            pltpu.VMEM((rows, d_model), jnp.bfloat16),
        ],
        compiler_params=pltpu.CompilerParams(
            dimension_semantics=("arbitrary",),
            vmem_limit_bytes=_MIXER_VMEM_BYTES,
        ),
        name="token_mixer",
    )(x, *consts, w_up, w_down)


def _ffn_kernel(h2_ref, x1_ref, wg_ref, wv_ref, cwg_ref, cwv_ref, bg_ref, bv_ref, wd_ref, gfin_ref,
                o_ref, gbuf, vbuf, gcarry, vcarry):
    i = pl.program_id(0)
    j = pl.program_id(1)
    rows = h2_ref.shape[0]

    @pl.when(i == 0)
    def _():
        gcarry[j] = jnp.zeros(gcarry.shape[1:], jnp.float32)
        vcarry[j] = jnp.zeros(vcarry.shape[1:], jnp.float32)

    @pl.when(j == 0)
    def _():
        o_ref[...] = x1_ref[...]

    h2 = h2_ref[...]
    halves = []
    for w_ref, cw_ref, b_ref, buf, carry in ((wg_ref, cwg_ref, bg_ref, gbuf, gcarry),
                                             (wv_ref, cwv_ref, bv_ref, vbuf, vcarry)):
        buf[0:CONV_HALO, :] = carry[j]
        buf[CONV_HALO:CONV_HALO + rows, :] = _dot(h2, w_ref[...])
        carry[j] = buf[rows:rows + CONV_HALO, :]
        halves.append(_causal_conv3(buf, cw_ref, rows) + b_ref[...])
    gate, val = halves
    act = (gate * jax.nn.sigmoid(gate) * val).astype(jnp.bfloat16)
    o_ref[...] += _dot(act, wd_ref[...])

    @pl.when(j == pl.num_programs(1) - 1)
    def _():
        o_ref[...] = _rmsnorm(o_ref[...], gfin_ref[...])


def _ffn(h2, x1, w_up, ffn_conv_w, ffn_conv_b, w_down, g_final):
    seq, d_model = x1.shape
    d_ff = w_down.shape[0]
    rows, cols = FFN_ROWS, FFN_COLS
    assert seq % rows == 0 and d_ff % cols == 0 and w_up.shape == (2 * d_ff // cols, d_model, cols)
    n_j = d_ff // cols
    row_spec = lambda: pl.BlockSpec((rows, d_model), lambda i, j: (i, 0))
    gate_cols = lambda r: pl.BlockSpec((r, cols), lambda i, j: (0, j))
    val_cols = lambda r: pl.BlockSpec((r, cols), lambda i, j: (0, j + n_j))
    up_gate = pl.BlockSpec((None, d_model, cols), lambda i, j: (j, 0, 0))
    up_val = pl.BlockSpec((None, d_model, cols), lambda i, j: (j + n_j, 0, 0))
    return pl.pallas_call(
        _ffn_kernel,
        grid=(seq // rows, n_j),
        in_specs=[
            row_spec(), row_spec(),
            up_gate, up_val,
            gate_cols(CONV_WIDTH), val_cols(CONV_WIDTH),
            gate_cols(1), val_cols(1),
            pl.BlockSpec((cols, d_model), lambda i, j: (j, 0)),
            pl.BlockSpec((1, d_model), lambda i, j: (0, 0)),
        ],
        out_specs=row_spec(),
        out_shape=jax.ShapeDtypeStruct((seq, d_model), jnp.float32),
        scratch_shapes=[
            pltpu.VMEM((rows + CONV_HALO, cols), jnp.float32),
            pltpu.VMEM((rows + CONV_HALO, cols), jnp.float32),
            pltpu.VMEM((n_j, CONV_HALO, cols), jnp.float32),
            pltpu.VMEM((n_j, CONV_HALO, cols), jnp.float32),
        ],
        compiler_params=pltpu.CompilerParams(
            dimension_semantics=("arbitrary", "arbitrary"),
            vmem_limit_bytes=_FFN_VMEM_BYTES,
        ),
        name="ffn",
    )(h2, x1, w_up, w_up, ffn_conv_w, ffn_conv_w, ffn_conv_b, ffn_conv_b, w_down, g_final)


_MIXER_VMEM_BYTES = 56 * 1024 * 1024
_FFN_VMEM_BYTES = 48 * 1024 * 1024


def _block_diag(pool_w):
    groups, dim, _ = pool_w.shape
    out = jnp.zeros((groups * dim, groups * dim), pool_w.dtype)
    for g in range(groups):
        out = out.at[g * dim:(g + 1) * dim, g * dim:(g + 1) * dim].set(pool_w[g])
    return out


def kernel(x, mem, g_mix, g_mem, w_in, conv_w, pool_w, pool_scale, w_kv, w_out, g_ffn, w_up,
           ffn_conv_w, ffn_conv_b, w_down, g_final):
    assert x.shape[0] == 1 and mem.shape[0] == 1 and w_in.shape[0] == 1
    assert tuple(sorted(POOL_WINDOWS)) == POOL_WINDOWS and pool_w.shape[1] == len(POOL_WINDOWS)
    bf16 = jnp.bfloat16
    d_conv = conv_w.shape[2]
    d_pool = pool_scale.shape[1]
    k, v = _kv_proj(mem[0], g_mem[0][None, :], w_kv[0].astype(bf16))
    x1, h2, w_up16, w_down16 = _mixer(
        x[0], g_mix[0][None, :], w_in[0].astype(bf16), conv_w[0], _block_diag(pool_w[0]).astype(bf16),
        pool_scale[0][None, :], k, v, w_out[0].astype(bf16), g_ffn[0][None, :], w_up[0], w_down[0],
        d_conv=d_conv, d_pool=d_pool)
    out = _ffn(h2, x1, w_up16, ffn_conv_w[0], ffn_conv_b[0][None, :], w_down16, g_final[None, :])
    return out[None]
```

```python
import functools
import math

import jax
import jax.numpy as jnp
from jax import lax
from jax.experimental import pallas as pl
from jax.experimental.pallas import tpu as pltpu

EPS = 1e-6
POOL_WINDOWS = (2, 4, 8, 16)
XATT_HEADS = 4
CONV_WIDTH = 3

V7X_SUBLANES = 8
V7X_LANES = 128
BF16_SUBLANES = 2 * V7X_SUBLANES
V7X_VMEM_BYTES = 64 * 1024 * 1024

MIX_ROWS = 256
FFN_ROWS = 1024
FFN_SUB_ROWS = 512
FFN_NORM_ROWS = 128
FFN_COLS = 512
POOL_HALO = 16
CONV_HALO = V7X_SUBLANES


def _rmsnorm(x, g):
    return x * lax.rsqrt(jnp.mean(x * x, axis=-1, keepdims=True) + EPS) * g


def _dot(a, b):
    return jnp.dot(a, b, preferred_element_type=jnp.float32)


def _causal_conv3(buf_ref, w_ref, rows):
    out = w_ref[2:3, :] * buf_ref[CONV_HALO:CONV_HALO + rows, :]
    for j in range(CONV_WIDTH - 1):
        lag = CONV_WIDTH - 1 - j
        out = out + w_ref[j:j + 1, :] * buf_ref[CONV_HALO - lag:CONV_HALO - lag + rows, :]
    return out


def _kv_kernel(mem_ref, g_ref, w_ref, k_ref, v_ref):
    m = _rmsnorm(mem_ref[...], g_ref[...]).astype(jnp.bfloat16)
    kv = _dot(m, w_ref[...])
    heads, _, head_dim = k_ref.shape
    for hd in range(heads):
        k_ref[hd] = kv[:, hd * head_dim:(hd + 1) * head_dim].astype(jnp.bfloat16)
        v_ref[hd] = kv[:, (heads + hd) * head_dim:(heads + hd + 1) * head_dim].astype(jnp.bfloat16)


def _kv_proj(mem, g_mem, w_kv):
    n_mem, _ = mem.shape
    head_dim = w_kv.shape[1] // (2 * XATT_HEADS)
    out = jax.ShapeDtypeStruct((XATT_HEADS, n_mem, head_dim), jnp.bfloat16)
    return pl.pallas_call(_kv_kernel, out_shape=(out, out), name="kv_proj")(mem, g_mem, w_kv)


def _mixer_kernel(x_ref, gmix_ref, win_ref, convw_ref, poolw_ref, pscale_ref, k_ref, v_ref,
                  wout_ref, gffn_ref, wup_ref, wdown_ref, x1_ref, h2_ref, wup16_ref, wdown16_ref,
                  zbuf, pbuf, vbuf, mixbuf, *, d_conv, d_pool, head_dim):
    i = pl.program_id(0)
    rows = x_ref.shape[0]

    ff_cols = wup16_ref.shape[2]
    for c in range(wup16_ref.shape[0]):
        wup16_ref[c] = wup_ref[:, c * ff_cols:(c + 1) * ff_cols].astype(jnp.bfloat16)
    wdown16_ref[...] = wdown_ref[...].astype(jnp.bfloat16)

    @pl.when(i == 0)
    def _():
        pbuf[0:CONV_HALO, :] = jnp.zeros((CONV_HALO, d_conv), jnp.float32)
        vbuf[0:POOL_HALO, :] = jnp.zeros((POOL_HALO, d_pool), jnp.float32)

    x = x_ref[...]
    h = _rmsnorm(x, gmix_ref[...]).astype(jnp.bfloat16)
    zbuf[...] = _dot(h, win_ref[...])

    c0 = d_conv
    pbuf[CONV_HALO:CONV_HALO + rows, :] = zbuf[:, c0:2 * c0] * zbuf[:, 2 * c0:3 * c0]
    conv = _causal_conv3(pbuf, convw_ref, rows)
    pbuf[0:CONV_HALO, :] = pbuf[rows:rows + CONV_HALO, :]
    mixbuf[:, 0:c0] = (zbuf[:, 0:c0] * conv).astype(jnp.bfloat16)

    p0 = 3 * d_conv
    vbuf[POOL_HALO:POOL_HALO + rows, :] = zbuf[:, p0:p0 + d_pool]
    group_dim = d_pool // len(POOL_WINDOWS)
    t = (i * rows + lax.broadcasted_iota(jnp.int32, (rows, 1), 0)).astype(jnp.float32)
    for c in range(0, d_pool, V7X_LANES):
        col = c + lax.broadcasted_iota(jnp.int32, (1, V7X_LANES), 1)
        cur = vbuf[POOL_HALO:POOL_HALO + rows, c:c + V7X_LANES]
        acc = cur
        lag = 1
        pooled = None
        for g in range(c // group_dim, (c + V7X_LANES - 1) // group_dim + 1):
            k = POOL_WINDOWS[g]
            while lag < k:
                acc = acc + vbuf[POOL_HALO - lag:POOL_HALO - lag + rows, c:c + V7X_LANES]
                lag += 1
            mean_k = acc / jnp.minimum(t + 1.0, float(k))
            pooled = mean_k if pooled is None else jnp.where(col >= g * group_dim, mean_k, pooled)
        mixbuf[:, c0 + c:c0 + c + V7X_LANES] = (pooled - cur).astype(jnp.bfloat16)
    vbuf[0:POOL_HALO, :] = vbuf[rows:rows + POOL_HALO, :]
    pool_out = _dot(mixbuf[:, c0:c0 + d_pool], poolw_ref[...]) * pscale_ref[...]
    mixbuf[:, c0:c0 + d_pool] = pool_out.astype(jnp.bfloat16)

    q0 = p0 + d_pool
    a0 = c0 + d_pool
    q = jnp.stack([zbuf[:, q0 + hd * head_dim:q0 + (hd + 1) * head_dim] for hd in range(XATT_HEADS)])
    s = jnp.einsum("hqd,hmd->hqm", q.astype(jnp.bfloat16), k_ref[...],
                   preferred_element_type=jnp.float32) * (1.0 / math.sqrt(head_dim))
    e = jnp.exp(s - jnp.max(s, axis=-1, keepdims=True))
    probs = (e / jnp.sum(e, axis=-1, keepdims=True)).astype(jnp.bfloat16)
    att = jnp.einsum("hqm,hmd->hqd", probs, v_ref[...], preferred_element_type=jnp.float32)
    for hd in range(XATT_HEADS):
        mixbuf[:, a0 + hd * head_dim:a0 + (hd + 1) * head_dim] = att[hd].astype(jnp.bfloat16)

    x1 = x + _dot(mixbuf[...], wout_ref[...])
    x1_ref[...] = x1
    h2_ref[...] = _rmsnorm(x1, gffn_ref[...]).astype(jnp.bfloat16)


def _const_spec(shape):
    return pl.BlockSpec(shape, lambda i: (0,) * len(shape), pipeline_mode=pl.Buffered(1))


def _mixer(x, g_mix, w_in, conv_w, pool_bd, pool_scale, k, v, w_out, g_ffn, w_up, w_down, *, d_conv, d_pool):
    seq, d_model = x.shape
    rows = MIX_ROWS
    steps = seq // rows
    assert seq % rows == 0 and rows % POOL_HALO == 0
    head_dim = k.shape[2]
    row_spec = pl.BlockSpec((rows, d_model), lambda i: (i, 0))
    consts = (g_mix, w_in, conv_w, pool_bd, pool_scale, k, v, w_out, g_ffn)
    d_ff = w_down.shape[0]
    up_chunks = w_up.shape[1] // FFN_COLS
    up_rows = d_model // steps
    down_rows = 2 * d_ff // steps
    assert d_model % steps == 0 and (2 * d_ff) % steps == 0 and steps % 2 == 0
    assert up_rows % BF16_SUBLANES == 0 and down_rows % BF16_SUBLANES == 0 and w_up.shape[1] % FFN_COLS == 0
    return pl.pallas_call(
        functools.partial(_mixer_kernel, d_conv=d_conv, d_pool=d_pool, head_dim=head_dim),
        grid=(steps,),
        in_specs=[row_spec] + [_const_spec(c.shape) for c in consts] + [
            pl.BlockSpec((up_rows, w_up.shape[1]), lambda i: (i, 0)),
            pl.BlockSpec((down_rows, d_model), lambda i: (i // 2, 0)),
        ],
        out_specs=(row_spec, row_spec,
                   pl.BlockSpec((up_chunks, up_rows, FFN_COLS), lambda i: (0, i, 0)),
                   pl.BlockSpec((down_rows, d_model), lambda i: (i // 2, 0))),
        out_shape=(jax.ShapeDtypeStruct((seq, d_model), jnp.float32),
                   jax.ShapeDtypeStruct((seq, d_model), jnp.bfloat16),
                   jax.ShapeDtypeStruct((up_chunks, d_model, FFN_COLS), jnp.bfloat16),
                   jax.ShapeDtypeStruct((d_ff, d_model), jnp.bfloat16)),
        scratch_shapes=[
            pltpu.VMEM((rows, w_in.shape[1]), jnp.float32),
            pltpu.VMEM((rows + CONV_HALO, d_conv), jnp.float32),
            pltpu.VMEM((rows + POOL_HALO, d_pool), jnp.float32),
            pltpu.VMEM((rows, d_model), jnp.bfloat16),
        ],
        compiler_params=pltpu.CompilerParams(
            dimension_semantics=("arbitrary",),
            vmem_limit_bytes=_MIXER_VMEM_BYTES,
        ),
        name="token_mixer",
    )(x, *consts, w_up, w_down)


def _ffn_kernel(h2_ref, x1_hbm, wg_ref, wv_ref, cwg_ref, cwv_ref, bg_ref, bv_ref, wd_ref, gfin_ref,
                o_ref, gbuf, vbuf, gcarry, vcarry, x1buf, x1_sem):
    i = pl.program_id(0)
    j = pl.program_id(1)
    last_j = pl.num_programs(1) - 1
    rows = h2_ref.shape[0]

    def x1_copy():
        return pltpu.make_async_copy(x1_hbm.at[pl.ds(i * rows, rows), :], x1buf, x1_sem)

    @pl.when(i == 0)
    def _():
        gcarry[j] = jnp.zeros(gcarry.shape[1:], jnp.float32)
        vcarry[j] = jnp.zeros(vcarry.shape[1:], jnp.float32)

    @pl.when(j == 0)
    def _():
        x1_copy().start()
        o_ref[...] = jnp.zeros(o_ref.shape, jnp.float32)

    @pl.when(j == last_j)
    def _():
        x1_copy().wait()

    gbuf[0:CONV_HALO, :] = gcarry[j]
    vbuf[0:CONV_HALO, :] = vcarry[j]
    for r in range(0, rows, FFN_SUB_ROWS):
        part = slice(r, r + FFN_SUB_ROWS)
        halo_part = slice(r, r + CONV_HALO + FFN_SUB_ROWS)
        h2 = h2_ref[part, :]
        gbuf[CONV_HALO + r:CONV_HALO + r + FFN_SUB_ROWS, :] = _dot(h2, wg_ref[...])
        vbuf[CONV_HALO + r:CONV_HALO + r + FFN_SUB_ROWS, :] = _dot(h2, wv_ref[...])
        gate = _causal_conv3(gbuf.at[halo_part, :], cwg_ref, FFN_SUB_ROWS) + bg_ref[...]
        val = _causal_conv3(vbuf.at[halo_part, :], cwv_ref, FFN_SUB_ROWS) + bv_ref[...]
        act = (gate * jax.nn.sigmoid(gate) * val).astype(jnp.bfloat16)
        o_ref[part, :] += _dot(act, wd_ref[...])
    gcarry[j] = gbuf[rows:rows + CONV_HALO, :]
    vcarry[j] = vbuf[rows:rows + CONV_HALO, :]

    @pl.when(j == last_j)
    def _():
        def norm_rows(c, carry):
            r = pl.multiple_of(c * FFN_NORM_ROWS, FFN_NORM_ROWS)
            y = x1buf[pl.ds(r, FFN_NORM_ROWS), :] + o_ref[pl.ds(r, FFN_NORM_ROWS), :]
            o_ref[pl.ds(r, FFN_NORM_ROWS), :] = _rmsnorm(y, gfin_ref[...])
            return carry

        lax.fori_loop(0, rows // FFN_NORM_ROWS, norm_rows, None)


def _ffn(h2, x1, w_up, ffn_conv_w, ffn_conv_b, w_down, g_final):
    seq, d_model = x1.shape
    d_ff = w_down.shape[0]
    rows, cols = FFN_ROWS, FFN_COLS
    assert seq % rows == 0 and d_ff % cols == 0 and w_up.shape == (2 * d_ff // cols, d_model, cols)
    assert rows % FFN_SUB_ROWS == 0 and rows % FFN_NORM_ROWS == 0
    n_j = d_ff // cols
    row_spec = lambda: pl.BlockSpec((rows, d_model), lambda i, j: (i, 0))
    gate_cols = lambda r: pl.BlockSpec((r, cols), lambda i, j: (0, j))
    val_cols = lambda r: pl.BlockSpec((r, cols), lambda i, j: (0, j + n_j))
    up_gate = pl.BlockSpec((None, d_model, cols), lambda i, j: (j, 0, 0))
    up_val = pl.BlockSpec((None, d_model, cols), lambda i, j: (j + n_j, 0, 0))
    return pl.pallas_call(
        _ffn_kernel,
        grid=(seq // rows, n_j),
        in_specs=[
            row_spec(), pl.BlockSpec(memory_space=pl.ANY),
            up_gate, up_val,
            gate_cols(CONV_WIDTH), val_cols(CONV_WIDTH),
            gate_cols(1), val_cols(1),
            pl.BlockSpec((cols, d_model), lambda i, j: (j, 0)),
            pl.BlockSpec((1, d_model), lambda i, j: (0, 0)),
        ],
        out_specs=row_spec(),
        out_shape=jax.ShapeDtypeStruct((seq, d_model), jnp.float32),
        scratch_shapes=[
            pltpu.VMEM((rows + CONV_HALO, cols), jnp.float32),
            pltpu.VMEM((rows + CONV_HALO, cols), jnp.float32),
            pltpu.VMEM((n_j, CONV_HALO, cols), jnp.float32),
            pltpu.VMEM((n_j, CONV_HALO, cols), jnp.float32),
            pltpu.VMEM((rows, d_model), jnp.float32),
            pltpu.SemaphoreType.DMA(()),
        ],
        compiler_params=pltpu.CompilerParams(
            dimension_semantics=("arbitrary", "arbitrary"),
            vmem_limit_bytes=_FFN_VMEM_BYTES,
        ),
        name="ffn",
    )(h2, x1, w_up, w_up, ffn_conv_w, ffn_conv_w, ffn_conv_b, ffn_conv_b, w_down, g_final)


_MIXER_VMEM_BYTES = 56 * 1024 * 1024
_FFN_VMEM_BYTES = 60 * 1024 * 1024


def _block_diag(pool_w):
    groups, dim, _ = pool_w.shape
    out = jnp.zeros((groups * dim, groups * dim), pool_w.dtype)
    for g in range(groups):
        out = out.at[g * dim:(g + 1) * dim, g * dim:(g + 1) * dim].set(pool_w[g])
    return out


def kernel(x, mem, g_mix, g_mem, w_in, conv_w, pool_w, pool_scale, w_kv, w_out, g_ffn, w_up,
           ffn_conv_w, ffn_conv_b, w_down, g_final):
    assert x.shape[0] == 1 and mem.shape[0] == 1 and w_in.shape[0] == 1
    assert tuple(sorted(POOL_WINDOWS)) == POOL_WINDOWS and pool_w.shape[1] == len(POOL_WINDOWS)
    bf16 = jnp.bfloat16
    d_conv = conv_w.shape[2]
    d_pool = pool_scale.shape[1]
    k, v = _kv_proj(mem[0], g_mem[0][None, :], w_kv[0].astype(bf16))
    x1, h2, w_up16, w_down16 = _mixer(
        x[0], g_mix[0][None, :], w_in[0].astype(bf16), conv_w[0], _block_diag(pool_w[0]).astype(bf16),
        pool_scale[0][None, :], k, v, w_out[0].astype(bf16), g_ffn[0][None, :], w_up[0], w_down[0],
        d_conv=d_conv, d_pool=d_pool)
    out = _ffn(h2, x1, w_up16, ffn_conv_w[0], ffn_conv_b[0][None, :], w_down16, g_final[None, :])
    return out[None]
```

```python
import functools
import math

import jax
import jax.numpy as jnp
from jax import lax
from jax.experimental import pallas as pl
from jax.experimental.pallas import tpu as pltpu

EPS = 1e-6
POOL_WINDOWS = (2, 4, 8, 16)
XATT_HEADS = 4
CONV_WIDTH = 3

V7X_SUBLANES = 8
V7X_LANES = 128
BF16_SUBLANES = 2 * V7X_SUBLANES
V7X_VMEM_BYTES = 64 * 1024 * 1024

MIX_ROWS = 256
FFN_ROWS = 1024
FFN_SUB_ROWS = 512
FFN_NORM_ROWS = 128
FFN_COLS = 512
POOL_HALO = 16
CONV_HALO = V7X_SUBLANES


def _rmsnorm(x, g):
    return x * lax.rsqrt(jnp.mean(x * x, axis=-1, keepdims=True) + EPS) * g


def _dot(a, b):
    return jnp.dot(a, b, preferred_element_type=jnp.float32)


def _causal_conv3(buf_ref, w_ref, rows):
    out = w_ref[2:3, :] * buf_ref[CONV_HALO:CONV_HALO + rows, :]
    for j in range(CONV_WIDTH - 1):
        lag = CONV_WIDTH - 1 - j
        out = out + w_ref[j:j + 1, :] * buf_ref[CONV_HALO - lag:CONV_HALO - lag + rows, :]
    return out


def _kv_kernel(mem_ref, g_ref, w_ref, k_ref, v_ref):
    m = _rmsnorm(mem_ref[...], g_ref[...]).astype(jnp.bfloat16)
    kv = _dot(m, w_ref[...])
    heads, _, head_dim = k_ref.shape
    for hd in range(heads):
        k_ref[hd] = kv[:, hd * head_dim:(hd + 1) * head_dim].astype(jnp.bfloat16)
        v_ref[hd] = kv[:, (heads + hd) * head_dim:(heads + hd + 1) * head_dim].astype(jnp.bfloat16)


def _kv_proj(mem, g_mem, w_kv):
    n_mem, _ = mem.shape
    head_dim = w_kv.shape[1] // (2 * XATT_HEADS)
    out = jax.ShapeDtypeStruct((XATT_HEADS, n_mem, head_dim), jnp.bfloat16)
    return pl.pallas_call(_kv_kernel, out_shape=(out, out), name="kv_proj")(mem, g_mem, w_kv)


def _mixer_kernel(x_ref, gmix_ref, win_ref, convw_ref, poolw_ref, pscale_ref, k_ref, v_ref,
                  wout_ref, gffn_ref, wup_ref, wdown_ref, x1_ref, h2_ref, wup16_ref, wdown16_ref,
                  zbuf, pbuf, vbuf, mixbuf, *, d_conv, d_pool, head_dim):
    i = pl.program_id(0)
    rows = x_ref.shape[0]

    ff_cols = wup16_ref.shape[2]
    for c in range(wup16_ref.shape[0]):
        wup16_ref[c] = wup_ref[:, c * ff_cols:(c + 1) * ff_cols].astype(jnp.bfloat16)
    wdown16_ref[...] = wdown_ref[...].astype(jnp.bfloat16)

    @pl.when(i == 0)
    def _():
        pbuf[0:CONV_HALO, :] = jnp.zeros((CONV_HALO, d_conv), jnp.float32)
        vbuf[0:POOL_HALO, :] = jnp.zeros((POOL_HALO, d_pool), jnp.float32)

    x = x_ref[...]
    h = _rmsnorm(x, gmix_ref[...]).astype(jnp.bfloat16)
    zbuf[...] = _dot(h, win_ref[...])

    c0 = d_conv
    pbuf[CONV_HALO:CONV_HALO + rows, :] = zbuf[:, c0:2 * c0] * zbuf[:, 2 * c0:3 * c0]
    conv = _causal_conv3(pbuf, convw_ref, rows)
    pbuf[0:CONV_HALO, :] = pbuf[rows:rows + CONV_HALO, :]
    mixbuf[:, 0:c0] = (zbuf[:, 0:c0] * conv).astype(jnp.bfloat16)

    p0 = 3 * d_conv
    vbuf[POOL_HALO:POOL_HALO + rows, :] = zbuf[:, p0:p0 + d_pool]
    group_dim = d_pool // len(POOL_WINDOWS)
    t = (i * rows + lax.broadcasted_iota(jnp.int32, (rows, 1), 0)).astype(jnp.float32)
    for c in range(0, d_pool, V7X_LANES):
        col = c + lax.broadcasted_iota(jnp.int32, (1, V7X_LANES), 1)
        cur = vbuf[POOL_HALO:POOL_HALO + rows, c:c + V7X_LANES]
        acc = cur
        lag = 1
        pooled = None
        for g in range(c // group_dim, (c + V7X_LANES - 1) // group_dim + 1):
            k = POOL_WINDOWS[g]
            while lag < k:
                acc = acc + vbuf[POOL_HALO - lag:POOL_HALO - lag + rows, c:c + V7X_LANES]
                lag += 1
            mean_k = acc / jnp.minimum(t + 1.0, float(k))
            pooled = mean_k if pooled is None else jnp.where(col >= g * group_dim, mean_k, pooled)
        mixbuf[:, c0 + c:c0 + c + V7X_LANES] = (pooled - cur).astype(jnp.bfloat16)
    vbuf[0:POOL_HALO, :] = vbuf[rows:rows + POOL_HALO, :]
    pool_out = _dot(mixbuf[:, c0:c0 + d_pool], poolw_ref[...]) * pscale_ref[...]
    mixbuf[:, c0:c0 + d_pool] = pool_out.astype(jnp.bfloat16)

    q0 = p0 + d_pool
    a0 = c0 + d_pool
    q = jnp.stack([zbuf[:, q0 + hd * head_dim:q0 + (hd + 1) * head_dim] for hd in range(XATT_HEADS)])
    s = jnp.einsum("hqd,hmd->hqm", q.astype(jnp.bfloat16), k_ref[...],
                   preferred_element_type=jnp.float32) * (1.0 / math.sqrt(head_dim))
    e = jnp.exp(s - jnp.max(s, axis=-1, keepdims=True))
    probs = (e / jnp.sum(e, axis=-1, keepdims=True)).astype(jnp.bfloat16)
    att = jnp.einsum("hqm,hmd->hqd", probs, v_ref[...], preferred_element_type=jnp.float32)
    for hd in range(XATT_HEADS):
        mixbuf[:, a0 + hd * head_dim:a0 + (hd + 1) * head_dim] = att[hd].astype(jnp.bfloat16)

    x1 = x + _dot(mixbuf[...], wout_ref[...])
    x1_ref[...] = x1
    h2_ref[...] = _rmsnorm(x1, gffn_ref[...]).astype(jnp.bfloat16)


def _const_spec(shape):
    return pl.BlockSpec(shape, lambda i: (0,) * len(shape), pipeline_mode=pl.Buffered(1))


def _mixer(x, g_mix, w_in, conv_w, pool_bd, pool_scale, k, v, w_out, g_ffn, w_up, w_down, *, d_conv, d_pool):
    seq, d_model = x.shape
    rows = MIX_ROWS
    steps = seq // rows
    assert seq % rows == 0 and rows % POOL_HALO == 0
    head_dim = k.shape[2]
    row_spec = pl.BlockSpec((rows, d_model), lambda i: (i, 0))
    consts = (g_mix, w_in, conv_w, pool_bd, pool_scale, k, v, w_out, g_ffn)
    d_ff = w_down.shape[0]
    up_chunks = w_up.shape[1] // FFN_COLS
    up_rows = d_model // steps
    down_rows = 2 * d_ff // steps
    assert d_model % steps == 0 and (2 * d_ff) % steps == 0 and steps % 2 == 0
    assert up_rows % BF16_SUBLANES == 0 and down_rows % BF16_SUBLANES == 0 and w_up.shape[1] % FFN_COLS == 0
    return pl.pallas_call(
        functools.partial(_mixer_kernel, d_conv=d_conv, d_pool=d_pool, head_dim=head_dim),
        grid=(steps,),
        in_specs=[row_spec] + [_const_spec(c.shape) for c in consts] + [
            pl.BlockSpec((up_rows, w_up.shape[1]), lambda i: (i, 0)),
            pl.BlockSpec((down_rows, d_model), lambda i: (i // 2, 0)),
        ],
        out_specs=(row_spec, row_spec,
                   pl.BlockSpec((up_chunks, up_rows, FFN_COLS), lambda i: (0, i, 0)),
                   pl.BlockSpec((down_rows, d_model), lambda i: (i // 2, 0))),
        out_shape=(jax.ShapeDtypeStruct((seq, d_model), jnp.float32),
                   jax.ShapeDtypeStruct((seq, d_model), jnp.bfloat16),
                   jax.ShapeDtypeStruct((up_chunks, d_model, FFN_COLS), jnp.bfloat16),
                   jax.ShapeDtypeStruct((d_ff, d_model), jnp.bfloat16)),
        scratch_shapes=[
            pltpu.VMEM((rows, w_in.shape[1]), jnp.float32),
            pltpu.VMEM((rows + CONV_HALO, d_conv), jnp.float32),
            pltpu.VMEM((rows + POOL_HALO, d_pool), jnp.float32),
            pltpu.VMEM((rows, d_model), jnp.bfloat16),
        ],
        compiler_params=pltpu.CompilerParams(
            dimension_semantics=("arbitrary",),
            vmem_limit_bytes=_MIXER_VMEM_BYTES,
        ),
        name="token_mixer",
    )(x, *consts, w_up, w_down)


def _ffn_kernel(h2_ref, x1_hbm, wg_ref, wv_ref, cwg_ref, cwv_ref, bg_ref, bv_ref, wd_ref, gfin_ref,
                o_ref, gbuf, vbuf, gcarry, vcarry, x1buf, x1_sem):
    i = pl.program_id(0)
    j = pl.program_id(1)
    last_j = pl.num_programs(1) - 1
    rows = h2_ref.shape[0]

    def x1_copy():
        return pltpu.make_async_copy(x1_hbm.at[pl.ds(i * rows, rows), :], x1buf, x1_sem)

    @pl.when(i == 0)
    def _():
        gcarry[j] = jnp.zeros(gcarry.shape[1:], jnp.float32)
        vcarry[j] = jnp.zeros(vcarry.shape[1:], jnp.float32)

    @pl.when(j == 0)
    def _():
        x1_copy().start()
        o_ref[...] = jnp.zeros(o_ref.shape, jnp.float32)

    @pl.when(j == last_j)
    def _():
        x1_copy().wait()

    gbuf[0:CONV_HALO, :] = gcarry[j]
    vbuf[0:CONV_HALO, :] = vcarry[j]
    for r in range(0, rows, FFN_SUB_ROWS):
        h2 = h2_ref[r:r + FFN_SUB_ROWS, :]
        gbuf[CONV_HALO + r:CONV_HALO + r + FFN_SUB_ROWS, :] = _dot(h2, wg_ref[...])
        vbuf[CONV_HALO + r:CONV_HALO + r + FFN_SUB_ROWS, :] = _dot(h2, wv_ref[...])
    for r in range(0, rows, FFN_SUB_ROWS):
        part = slice(r, r + FFN_SUB_ROWS)
        halo_part = slice(r, r + CONV_HALO + FFN_SUB_ROWS)
        gate = _causal_conv3(gbuf.at[halo_part, :], cwg_ref, FFN_SUB_ROWS) + bg_ref[...]
        val = _causal_conv3(vbuf.at[halo_part, :], cwv_ref, FFN_SUB_ROWS) + bv_ref[...]
        act = (gate * jax.nn.sigmoid(gate) * val).astype(jnp.bfloat16)
        o_ref[part, :] += _dot(act, wd_ref[...])
    gcarry[j] = gbuf[rows:rows + CONV_HALO, :]
    vcarry[j] = vbuf[rows:rows + CONV_HALO, :]

    @pl.when(j == last_j)
    def _():
        def norm_rows(c, carry):
            r = pl.multiple_of(c * FFN_NORM_ROWS, FFN_NORM_ROWS)
            y = x1buf[pl.ds(r, FFN_NORM_ROWS), :] + o_ref[pl.ds(r, FFN_NORM_ROWS), :]
            o_ref[pl.ds(r, FFN_NORM_ROWS), :] = _rmsnorm(y, gfin_ref[...])
            return carry

        lax.fori_loop(0, rows // FFN_NORM_ROWS, norm_rows, None)


def _ffn(h2, x1, w_up, ffn_conv_w, ffn_conv_b, w_down, g_final):
    seq, d_model = x1.shape
    d_ff = w_down.shape[0]
    rows, cols = FFN_ROWS, FFN_COLS
    assert seq % rows == 0 and d_ff % cols == 0 and w_up.shape == (2 * d_ff // cols, d_model, cols)
    assert rows % FFN_SUB_ROWS == 0 and rows % FFN_NORM_ROWS == 0
    n_j = d_ff // cols
    row_spec = lambda: pl.BlockSpec((rows, d_model), lambda i, j: (i, 0))
    gate_cols = lambda r: pl.BlockSpec((r, cols), lambda i, j: (0, j))
    val_cols = lambda r: pl.BlockSpec((r, cols), lambda i, j: (0, j + n_j))
    up_gate = pl.BlockSpec((None, d_model, cols), lambda i, j: (j, 0, 0))
    up_val = pl.BlockSpec((None, d_model, cols), lambda i, j: (j + n_j, 0, 0))
    return pl.pallas_call(
        _ffn_kernel,
        grid=(seq // rows, n_j),
        in_specs=[
            row_spec(), pl.BlockSpec(memory_space=pl.ANY),
            up_gate, up_val,
            gate_cols(CONV_WIDTH), val_cols(CONV_WIDTH),
            gate_cols(1), val_cols(1),
            pl.BlockSpec((cols, d_model), lambda i, j: (j, 0)),
            pl.BlockSpec((1, d_model), lambda i, j: (0, 0)),
        ],
        out_specs=row_spec(),
        out_shape=jax.ShapeDtypeStruct((seq, d_model), jnp.float32),
        scratch_shapes=[
            pltpu.VMEM((rows + CONV_HALO, cols), jnp.float32),
            pltpu.VMEM((rows + CONV_HALO, cols), jnp.float32),
            pltpu.VMEM((n_j, CONV_HALO, cols), jnp.float32),
            pltpu.VMEM((n_j, CONV_HALO, cols), jnp.float32),
            pltpu.VMEM((rows, d_model), jnp.float32),
            pltpu.SemaphoreType.DMA(()),
        ],
        compiler_params=pltpu.CompilerParams(
            dimension_semantics=("arbitrary", "arbitrary"),
            vmem_limit_bytes=_FFN_VMEM_BYTES,
        ),
        name="ffn",
    )(h2, x1, w_up, w_up, ffn_conv_w, ffn_conv_w, ffn_conv_b, ffn_conv_b, w_down, g_final)


_MIXER_VMEM_BYTES = 56 * 1024 * 1024
_FFN_VMEM_BYTES = 60 * 1024 * 1024


def _block_diag(pool_w):
    groups, dim, _ = pool_w.shape
    out = jnp.zeros((groups * dim, groups * dim), pool_w.dtype)
    for g in range(groups):
        out = out.at[g * dim:(g + 1) * dim, g * dim:(g + 1) * dim].set(pool_w[g])
    return out


def kernel(x, mem, g_mix, g_mem, w_in, conv_w, pool_w, pool_scale, w_kv, w_out, g_ffn, w_up,
           ffn_conv_w, ffn_conv_b, w_down, g_final):
    assert x.shape[0] == 1 and mem.shape[0] == 1 and w_in.shape[0] == 1
    assert tuple(sorted(POOL_WINDOWS)) == POOL_WINDOWS and pool_w.shape[1] == len(POOL_WINDOWS)
    bf16 = jnp.bfloat16
    d_conv = conv_w.shape[2]
    d_pool = pool_scale.shape[1]
    k, v = _kv_proj(mem[0], g_mem[0][None, :], w_kv[0].astype(bf16))
    x1, h2, w_up16, w_down16 = _mixer(
        x[0], g_mix[0][None, :], w_in[0].astype(bf16), conv_w[0], _block_diag(pool_w[0]).astype(bf16),
        pool_scale[0][None, :], k, v, w_out[0].astype(bf16), g_ffn[0][None, :], w_up[0], w_down[0],
        d_conv=d_conv, d_pool=d_pool)
    out = _ffn(h2, x1, w_up16, ffn_conv_w[0], ffn_conv_b[0][None, :], w_down16, g_final[None, :])
    return out[None]
```

```python
import functools
import math

import jax
import jax.numpy as jnp
from jax import lax
from jax.experimental import pallas as pl
from jax.experimental.pallas import tpu as pltpu

EPS = 1e-6
POOL_WINDOWS = (2, 4, 8, 16)
XATT_HEADS = 4
CONV_WIDTH = 3

V7X_SUBLANES = 8
V7X_LANES = 128
BF16_SUBLANES = 2 * V7X_SUBLANES
V7X_VMEM_BYTES = 64 * 1024 * 1024

MIX_ROWS = 256
FFN_ROWS = 1024
FFN_SUB_ROWS = 512
FFN_NORM_ROWS = 128
FFN_COLS = 512
POOL_HALO = 16
CONV_HALO = V7X_SUBLANES


def _rmsnorm(x, g):
    return x * lax.rsqrt(jnp.mean(x * x, axis=-1, keepdims=True) + EPS) * g


def _dot(a, b):
    return jnp.dot(a, b, preferred_element_type=jnp.float32)


def _causal_conv3(buf_ref, w_ref, rows):
    out = w_ref[2:3, :] * buf_ref[CONV_HALO:CONV_HALO + rows, :]
    for j in range(CONV_WIDTH - 1):
        lag = CONV_WIDTH - 1 - j
        out = out + w_ref[j:j + 1, :] * buf_ref[CONV_HALO - lag:CONV_HALO - lag + rows, :]
    return out


def _kv_kernel(mem_ref, g_ref, w_ref, k_ref, v_ref):
    m = _rmsnorm(mem_ref[...], g_ref[...]).astype(jnp.bfloat16)
    kv = _dot(m, w_ref[...])
    heads, _, head_dim = k_ref.shape
    for hd in range(heads):
        k_ref[hd] = kv[:, hd * head_dim:(hd + 1) * head_dim].astype(jnp.bfloat16)
        v_ref[hd] = kv[:, (heads + hd) * head_dim:(heads + hd + 1) * head_dim].astype(jnp.bfloat16)


def _kv_proj(mem, g_mem, w_kv):
    n_mem, _ = mem.shape
    head_dim = w_kv.shape[1] // (2 * XATT_HEADS)
    out = jax.ShapeDtypeStruct((XATT_HEADS, n_mem, head_dim), jnp.bfloat16)
    return pl.pallas_call(_kv_kernel, out_shape=(out, out), name="kv_proj")(mem, g_mem, w_kv)


def _mixer_kernel(x_ref, gmix_ref, win_ref, convw_ref, poolw_ref, pscale_ref, k_ref, v_ref,
                  wout_ref, gffn_ref, wup_ref, wdown_ref, x1_ref, h2_ref, wup16_ref, wdown16_ref,
                  zbuf, pbuf, vbuf, mixbuf, *, d_conv, d_pool, head_dim):
    i = pl.program_id(0)
    rows = x_ref.shape[0]

    @pl.when(i == 0)
    def _():
        pbuf[0:CONV_HALO, :] = jnp.zeros((CONV_HALO, d_conv), jnp.float32)
        vbuf[0:POOL_HALO, :] = jnp.zeros((POOL_HALO, d_pool), jnp.float32)

    x = x_ref[...]
    h = _rmsnorm(x, gmix_ref[...]).astype(jnp.bfloat16)
    zbuf[...] = _dot(h, win_ref[...])

    def cast_up_chunks(lo, hi):
        ff_cols = wup16_ref.shape[2]
        for c in range(lo, hi):
            wup16_ref[c] = wup_ref[:, c * ff_cols:(c + 1) * ff_cols].astype(jnp.bfloat16)

    up_chunks = wup16_ref.shape[0]
    cast_up_chunks(0, up_chunks // 3)

    c0 = d_conv
    pbuf[CONV_HALO:CONV_HALO + rows, :] = zbuf[:, c0:2 * c0] * zbuf[:, 2 * c0:3 * c0]
    conv = _causal_conv3(pbuf, convw_ref, rows)
    pbuf[0:CONV_HALO, :] = pbuf[rows:rows + CONV_HALO, :]
    mixbuf[:, 0:c0] = (zbuf[:, 0:c0] * conv).astype(jnp.bfloat16)

    cast_up_chunks(up_chunks // 3, 2 * up_chunks // 3)

    p0 = 3 * d_conv
    vbuf[POOL_HALO:POOL_HALO + rows, :] = zbuf[:, p0:p0 + d_pool]
    group_dim = d_pool // len(POOL_WINDOWS)
    t = (i * rows + lax.broadcasted_iota(jnp.int32, (rows, 1), 0)).astype(jnp.float32)
    for c in range(0, d_pool, V7X_LANES):
        col = c + lax.broadcasted_iota(jnp.int32, (1, V7X_LANES), 1)
        cur = vbuf[POOL_HALO:POOL_HALO + rows, c:c + V7X_LANES]
        acc = cur
        lag = 1
        pooled = None
        for g in range(c // group_dim, (c + V7X_LANES - 1) // group_dim + 1):
            k = POOL_WINDOWS[g]
            while lag < k:
                acc = acc + vbuf[POOL_HALO - lag:POOL_HALO - lag + rows, c:c + V7X_LANES]
                lag += 1
            mean_k = acc / jnp.minimum(t + 1.0, float(k))
            pooled = mean_k if pooled is None else jnp.where(col >= g * group_dim, mean_k, pooled)
        mixbuf[:, c0 + c:c0 + c + V7X_LANES] = (pooled - cur).astype(jnp.bfloat16)
    vbuf[0:POOL_HALO, :] = vbuf[rows:rows + POOL_HALO, :]
    pool_out = _dot(mixbuf[:, c0:c0 + d_pool], poolw_ref[...]) * pscale_ref[...]
    mixbuf[:, c0:c0 + d_pool] = pool_out.astype(jnp.bfloat16)

    cast_up_chunks(2 * up_chunks // 3, up_chunks)

    q0 = p0 + d_pool
    a0 = c0 + d_pool
    q = jnp.stack([zbuf[:, q0 + hd * head_dim:q0 + (hd + 1) * head_dim] for hd in range(XATT_HEADS)])
    s = jnp.einsum("hqd,hmd->hqm", q.astype(jnp.bfloat16), k_ref[...],
                   preferred_element_type=jnp.float32) * (1.0 / math.sqrt(head_dim))
    e = jnp.exp(s - jnp.max(s, axis=-1, keepdims=True))
    probs = (e / jnp.sum(e, axis=-1, keepdims=True)).astype(jnp.bfloat16)
    att = jnp.einsum("hqm,hmd->hqd", probs, v_ref[...], preferred_element_type=jnp.float32)
    for hd in range(XATT_HEADS):
        mixbuf[:, a0 + hd * head_dim:a0 + (hd + 1) * head_dim] = att[hd].astype(jnp.bfloat16)

    wdown16_ref[...] = wdown_ref[...].astype(jnp.bfloat16)
    x1 = x + _dot(mixbuf[...], wout_ref[...])
    x1_ref[...] = x1
    h2_ref[...] = _rmsnorm(x1, gffn_ref[...]).astype(jnp.bfloat16)


def _const_spec(shape):
    return pl.BlockSpec(shape, lambda i: (0,) * len(shape), pipeline_mode=pl.Buffered(1))


def _mixer(x, g_mix, w_in, conv_w, pool_bd, pool_scale, k, v, w_out, g_ffn, w_up, w_down, *, d_conv, d_pool):
    seq, d_model = x.shape
    rows = MIX_ROWS
    steps = seq // rows
    assert seq % rows == 0 and rows % POOL_HALO == 0
    head_dim = k.shape[2]
    row_spec = pl.BlockSpec((rows, d_model), lambda i: (i, 0))
    consts = (g_mix, w_in, conv_w, pool_bd, pool_scale, k, v, w_out, g_ffn)
    d_ff = w_down.shape[0]
    up_chunks = w_up.shape[1] // FFN_COLS
    up_rows = d_model // steps
    down_rows = 2 * d_ff // steps
    assert d_model % steps == 0 and (2 * d_ff) % steps == 0 and steps % 2 == 0
    assert up_rows % BF16_SUBLANES == 0 and down_rows % BF16_SUBLANES == 0 and w_up.shape[1] % FFN_COLS == 0
    return pl.pallas_call(
        functools.partial(_mixer_kernel, d_conv=d_conv, d_pool=d_pool, head_dim=head_dim),
        grid=(steps,),
        in_specs=[row_spec] + [_const_spec(c.shape) for c in consts] + [
            pl.BlockSpec((up_rows, w_up.shape[1]), lambda i: (i, 0)),
            pl.BlockSpec((down_rows, d_model), lambda i: (i // 2, 0)),
        ],
        out_specs=(row_spec, row_spec,
                   pl.BlockSpec((up_chunks, up_rows, FFN_COLS), lambda i: (0, i, 0)),
                   pl.BlockSpec((down_rows, d_model), lambda i: (i // 2, 0))),
        out_shape=(jax.ShapeDtypeStruct((seq, d_model), jnp.float32),
                   jax.ShapeDtypeStruct((seq, d_model), jnp.bfloat16),
                   jax.ShapeDtypeStruct((up_chunks, d_model, FFN_COLS), jnp.bfloat16),
                   jax.ShapeDtypeStruct((d_ff, d_model), jnp.bfloat16)),
        scratch_shapes=[
            pltpu.VMEM((rows, w_in.shape[1]), jnp.float32),
            pltpu.VMEM((rows + CONV_HALO, d_conv), jnp.float32),
            pltpu.VMEM((rows + POOL_HALO, d_pool), jnp.float32),
            pltpu.VMEM((rows, d_model), jnp.bfloat16),
        ],
        compiler_params=pltpu.CompilerParams(
            dimension_semantics=("arbitrary",),
            vmem_limit_bytes=_MIXER_VMEM_BYTES,
        ),
        name="token_mixer",
    )(x, *consts, w_up, w_down)


def _ffn_kernel(h2_ref, x1_hbm, wg_ref, wv_ref, cwg_ref, cwv_ref, bg_ref, bv_ref, wd_ref, gfin_ref,
                o_ref, gbuf, vbuf, gcarry, vcarry, x1buf, x1_sem):
    i = pl.program_id(0)
    j = pl.program_id(1)
    last_j = pl.num_programs(1) - 1
    rows = h2_ref.shape[0]

    def x1_copy():
        return pltpu.make_async_copy(x1_hbm.at[pl.ds(i * rows, rows), :], x1buf, x1_sem)

    @pl.when(i == 0)
    def _():
        gcarry[j] = jnp.zeros(gcarry.shape[1:], jnp.float32)
        vcarry[j] = jnp.zeros(vcarry.shape[1:], jnp.float32)

    @pl.when(j == 0)
    def _():
        x1_copy().start()
        o_ref[...] = jnp.zeros(o_ref.shape, jnp.float32)

    @pl.when(j == last_j)
    def _():
        x1_copy().wait()

    gbuf[0:CONV_HALO, :] = gcarry[j]
    vbuf[0:CONV_HALO, :] = vcarry[j]
    for w_ref, buf in ((wg_ref, gbuf), (wv_ref, vbuf)):
        for r in range(0, rows, FFN_SUB_ROWS):
            buf[CONV_HALO + r:CONV_HALO + r + FFN_SUB_ROWS, :] = _dot(h2_ref[r:r + FFN_SUB_ROWS, :], w_ref[...])
    for r in range(0, rows, FFN_SUB_ROWS):
        part = slice(r, r + FFN_SUB_ROWS)
        halo_part = slice(r, r + CONV_HALO + FFN_SUB_ROWS)
        gate = _causal_conv3(gbuf.at[halo_part, :], cwg_ref, FFN_SUB_ROWS) + bg_ref[...]
        val = _causal_conv3(vbuf.at[halo_part, :], cwv_ref, FFN_SUB_ROWS) + bv_ref[...]
        act = (gate * jax.nn.sigmoid(gate) * val).astype(jnp.bfloat16)
        o_ref[part, :] += _dot(act, wd_ref[...])
    gcarry[j] = gbuf[rows:rows + CONV_HALO, :]
    vcarry[j] = vbuf[rows:rows + CONV_HALO, :]

    @pl.when(j == last_j)
    def _():
        def norm_rows(c, carry):
            r = pl.multiple_of(c * FFN_NORM_ROWS, FFN_NORM_ROWS)
            y = x1buf[pl.ds(r, FFN_NORM_ROWS), :] + o_ref[pl.ds(r, FFN_NORM_ROWS), :]
            o_ref[pl.ds(r, FFN_NORM_ROWS), :] = _rmsnorm(y, gfin_ref[...])
            return carry

        lax.fori_loop(0, rows // FFN_NORM_ROWS, norm_rows, None)


def _ffn(h2, x1, w_up, ffn_conv_w, ffn_conv_b, w_down, g_final):
    seq, d_model = x1.shape
    d_ff = w_down.shape[0]
    rows, cols = FFN_ROWS, FFN_COLS
    assert seq % rows == 0 and d_ff % cols == 0 and w_up.shape == (2 * d_ff // cols, d_model, cols)
    assert rows % FFN_SUB_ROWS == 0 and rows % FFN_NORM_ROWS == 0
    n_j = d_ff // cols
    row_spec = lambda: pl.BlockSpec((rows, d_model), lambda i, j: (i, 0))
    gate_cols = lambda r: pl.BlockSpec((r, cols), lambda i, j: (0, j))
    val_cols = lambda r: pl.BlockSpec((r, cols), lambda i, j: (0, j + n_j))
    up_gate = pl.BlockSpec((None, d_model, cols), lambda i, j: (j, 0, 0))
    up_val = pl.BlockSpec((None, d_model, cols), lambda i, j: (j + n_j, 0, 0))
    return pl.pallas_call(
        _ffn_kernel,
        grid=(seq // rows, n_j),
        in_specs=[
            row_spec(), pl.BlockSpec(memory_space=pl.ANY),
            up_gate, up_val,
            gate_cols(CONV_WIDTH), val_cols(CONV_WIDTH),
            gate_cols(1), val_cols(1),
            pl.BlockSpec((cols, d_model), lambda i, j: (j, 0)),
            pl.BlockSpec((1, d_model), lambda i, j: (0, 0)),
        ],
        out_specs=row_spec(),
        out_shape=jax.ShapeDtypeStruct((seq, d_model), jnp.float32),
        scratch_shapes=[
            pltpu.VMEM((rows + CONV_HALO, cols), jnp.float32),
            pltpu.VMEM((rows + CONV_HALO, cols), jnp.float32),
            pltpu.VMEM((n_j, CONV_HALO, cols), jnp.float32),
            pltpu.VMEM((n_j, CONV_HALO, cols), jnp.float32),
            pltpu.VMEM((rows, d_model), jnp.float32),
            pltpu.SemaphoreType.DMA(()),
        ],
        compiler_params=pltpu.CompilerParams(
            dimension_semantics=("arbitrary", "arbitrary"),
            vmem_limit_bytes=_FFN_VMEM_BYTES,
        ),
        name="ffn",
    )(h2, x1, w_up, w_up, ffn_conv_w, ffn_conv_w, ffn_conv_b, ffn_conv_b, w_down, g_final)


_MIXER_VMEM_BYTES = 56 * 1024 * 1024
_FFN_VMEM_BYTES = 60 * 1024 * 1024


def _block_diag(pool_w):
    groups, dim, _ = pool_w.shape
    bands = [jnp.pad(pool_w[g], ((0, 0), (g * dim, (groups - 1 - g) * dim))) for g in range(groups)]
    return jnp.concatenate(bands, axis=0)


def kernel(x, mem, g_mix, g_mem, w_in, conv_w, pool_w, pool_scale, w_kv, w_out, g_ffn, w_up,
           ffn_conv_w, ffn_conv_b, w_down, g_final):
    assert x.shape[0] == 1 and mem.shape[0] == 1 and w_in.shape[0] == 1
    assert tuple(sorted(POOL_WINDOWS)) == POOL_WINDOWS and pool_w.shape[1] == len(POOL_WINDOWS)
    bf16 = jnp.bfloat16
    d_conv = conv_w.shape[2]
    d_pool = pool_scale.shape[1]
    k, v = _kv_proj(mem[0], g_mem[0][None, :], w_kv[0].astype(bf16))
    x1, h2, w_up16, w_down16 = _mixer(
        x[0], g_mix[0][None, :], w_in[0].astype(bf16), conv_w[0], _block_diag(pool_w[0]).astype(bf16),
        pool_scale[0][None, :], k, v, w_out[0].astype(bf16), g_ffn[0][None, :], w_up[0], w_down[0],
        d_conv=d_conv, d_pool=d_pool)
    out = _ffn(h2, x1, w_up16, ffn_conv_w[0], ffn_conv_b[0][None, :], w_down16, g_final[None, :])
    return out[None]
```

```python
import functools
import math

import jax
import jax.numpy as jnp
from jax import lax
from jax.experimental import pallas as pl
from jax.experimental.pallas import tpu as pltpu

EPS = 1e-6
POOL_WINDOWS = (2, 4, 8, 16)
XATT_HEADS = 4
CONV_WIDTH = 3

V7X_SUBLANES = 8
V7X_LANES = 128
BF16_SUBLANES = 2 * V7X_SUBLANES
V7X_VMEM_BYTES = 64 * 1024 * 1024

MIX_ROWS = 256
FFN_ROWS = 1024
FFN_SUB_ROWS = 512
FFN_NORM_ROWS = 128
FFN_COLS = 512
POOL_HALO = 16
CONV_HALO = V7X_SUBLANES


def _rmsnorm(x, g):
    return x * lax.rsqrt(jnp.mean(x * x, axis=-1, keepdims=True) + EPS) * g


def _dot(a, b):
    return jnp.dot(a, b, preferred_element_type=jnp.float32)


def _causal_conv3(buf_ref, w_ref, rows):
    out = w_ref[2:3, :] * buf_ref[CONV_HALO:CONV_HALO + rows, :]
    for j in range(CONV_WIDTH - 1):
        lag = CONV_WIDTH - 1 - j
        out = out + w_ref[j:j + 1, :] * buf_ref[CONV_HALO - lag:CONV_HALO - lag + rows, :]
    return out


def _kv_kernel(mem_ref, g_ref, w_ref, k_ref, v_ref):
    m = _rmsnorm(mem_ref[...], g_ref[...]).astype(jnp.bfloat16)
    kv = _dot(m, w_ref[...].astype(jnp.bfloat16))
    heads, _, head_dim = k_ref.shape
    for hd in range(heads):
        k_ref[hd] = kv[:, hd * head_dim:(hd + 1) * head_dim].astype(jnp.bfloat16)
        v_ref[hd] = kv[:, (heads + hd) * head_dim:(heads + hd + 1) * head_dim].astype(jnp.bfloat16)


def _kv_proj(mem, g_mem, w_kv):
    n_mem, _ = mem.shape
    head_dim = w_kv.shape[1] // (2 * XATT_HEADS)
    out = jax.ShapeDtypeStruct((XATT_HEADS, n_mem, head_dim), jnp.bfloat16)
    return pl.pallas_call(_kv_kernel, out_shape=(out, out), name="kv_proj")(mem, g_mem, w_kv)


def _mixer_kernel(x_ref, gmix_ref, win_ref, convw_ref, poolw_ref, pscale_ref, k_ref, v_ref,
                  wout_ref, gffn_ref, wup_ref, wdown_ref, x1_ref, h2_ref, wup16_ref, wdown16_ref,
                  zbuf, pbuf, vbuf, mixbuf, *, d_conv, d_pool, head_dim):
    i = pl.program_id(0)
    rows = x_ref.shape[0]

    @pl.when(i == 0)
    def _():
        pbuf[0:CONV_HALO, :] = jnp.zeros((CONV_HALO, d_conv), jnp.float32)
        vbuf[0:POOL_HALO, :] = jnp.zeros((POOL_HALO, d_pool), jnp.float32)

    x = x_ref[...]
    h = _rmsnorm(x, gmix_ref[...]).astype(jnp.bfloat16)
    zbuf[...] = _dot(h, win_ref[...])

    def cast_up_chunks(lo, hi):
        ff_cols = wup16_ref.shape[2]
        for c in range(lo, hi):
            wup16_ref[c] = wup_ref[:, c * ff_cols:(c + 1) * ff_cols].astype(jnp.bfloat16)

    up_chunks = wup16_ref.shape[0]
    cast_up_chunks(0, up_chunks // 3)

    c0 = d_conv
    pbuf[CONV_HALO:CONV_HALO + rows, :] = zbuf[:, c0:2 * c0] * zbuf[:, 2 * c0:3 * c0]
    conv = _causal_conv3(pbuf, convw_ref, rows)
    pbuf[0:CONV_HALO, :] = pbuf[rows:rows + CONV_HALO, :]
    mixbuf[:, 0:c0] = (zbuf[:, 0:c0] * conv).astype(jnp.bfloat16)

    cast_up_chunks(up_chunks // 3, 2 * up_chunks // 3)

    p0 = 3 * d_conv
    vbuf[POOL_HALO:POOL_HALO + rows, :] = zbuf[:, p0:p0 + d_pool]
    group_dim = d_pool // len(POOL_WINDOWS)
    t = (i * rows + lax.broadcasted_iota(jnp.int32, (rows, 1), 0)).astype(jnp.float32)
    for c in range(0, d_pool, V7X_LANES):
        col = c + lax.broadcasted_iota(jnp.int32, (1, V7X_LANES), 1)
        cur = vbuf[POOL_HALO:POOL_HALO + rows, c:c + V7X_LANES]
        acc = cur
        lag = 1
        pooled = None
        for g in range(c // group_dim, (c + V7X_LANES - 1) // group_dim + 1):
            k = POOL_WINDOWS[g]
            while lag < k:
                acc = acc + vbuf[POOL_HALO - lag:POOL_HALO - lag + rows, c:c + V7X_LANES]
                lag += 1
            mean_k = acc / jnp.minimum(t + 1.0, float(k))
            pooled = mean_k if pooled is None else jnp.where(col >= g * group_dim, mean_k, pooled)
        mixbuf[:, c0 + c:c0 + c + V7X_LANES] = (pooled - cur).astype(jnp.bfloat16)
    vbuf[0:POOL_HALO, :] = vbuf[rows:rows + POOL_HALO, :]
    pool_out = _dot(mixbuf[:, c0:c0 + d_pool], poolw_ref[...]) * pscale_ref[...]
    mixbuf[:, c0:c0 + d_pool] = pool_out.astype(jnp.bfloat16)

    cast_up_chunks(2 * up_chunks // 3, up_chunks)

    q0 = p0 + d_pool
    a0 = c0 + d_pool
    q = jnp.stack([zbuf[:, q0 + hd * head_dim:q0 + (hd + 1) * head_dim] for hd in range(XATT_HEADS)])
    s = jnp.einsum("hqd,hmd->hqm", q.astype(jnp.bfloat16), k_ref[...],
                   preferred_element_type=jnp.float32) * (1.0 / math.sqrt(head_dim))
    e = jnp.exp(s - jnp.max(s, axis=-1, keepdims=True))
    probs = (e / jnp.sum(e, axis=-1, keepdims=True)).astype(jnp.bfloat16)
    att = jnp.einsum("hqm,hmd->hqd", probs, v_ref[...], preferred_element_type=jnp.float32)
    for hd in range(XATT_HEADS):
        mixbuf[:, a0 + hd * head_dim:a0 + (hd + 1) * head_dim] = att[hd].astype(jnp.bfloat16)

    wdown16_ref[...] = wdown_ref[...].astype(jnp.bfloat16)
    x1 = x + _dot(mixbuf[...], wout_ref[...])
    x1_ref[...] = x1
    h2_ref[...] = _rmsnorm(x1, gffn_ref[...]).astype(jnp.bfloat16)


def _const_spec(shape):
    return pl.BlockSpec(shape, lambda i: (0,) * len(shape), pipeline_mode=pl.Buffered(1))


def _mixer(x, g_mix, w_in, conv_w, pool_bd, pool_scale, k, v, w_out, g_ffn, w_up, w_down, *, d_conv, d_pool):
    seq, d_model = x.shape
    rows = MIX_ROWS
    steps = seq // rows
    assert seq % rows == 0 and rows % POOL_HALO == 0
    head_dim = k.shape[2]
    row_spec = pl.BlockSpec((rows, d_model), lambda i: (i, 0))
    consts = (g_mix, w_in, conv_w, pool_bd, pool_scale, k, v, w_out, g_ffn)
    d_ff = w_down.shape[0]
    up_chunks = w_up.shape[1] // FFN_COLS
    up_rows = d_model // steps
    down_rows = 2 * d_ff // steps
    assert d_model % steps == 0 and (2 * d_ff) % steps == 0 and steps % 2 == 0
    assert up_rows % BF16_SUBLANES == 0 and down_rows % BF16_SUBLANES == 0 and w_up.shape[1] % FFN_COLS == 0
    return pl.pallas_call(
        functools.partial(_mixer_kernel, d_conv=d_conv, d_pool=d_pool, head_dim=head_dim),
        grid=(steps,),
        in_specs=[row_spec] + [_const_spec(c.shape) for c in consts] + [
            pl.BlockSpec((up_rows, w_up.shape[1]), lambda i: (i, 0)),
            pl.BlockSpec((down_rows, d_model), lambda i: (i // 2, 0)),
        ],
        out_specs=(row_spec, row_spec,
                   pl.BlockSpec((up_chunks, up_rows, FFN_COLS), lambda i: (0, i, 0)),
                   pl.BlockSpec((down_rows, d_model), lambda i: (i // 2, 0))),
        out_shape=(jax.ShapeDtypeStruct((seq, d_model), jnp.float32),
                   jax.ShapeDtypeStruct((seq, d_model), jnp.bfloat16),
                   jax.ShapeDtypeStruct((up_chunks, d_model, FFN_COLS), jnp.bfloat16),
                   jax.ShapeDtypeStruct((d_ff, d_model), jnp.bfloat16)),
        scratch_shapes=[
            pltpu.VMEM((rows, w_in.shape[1]), jnp.float32),
            pltpu.VMEM((rows + CONV_HALO, d_conv), jnp.float32),
            pltpu.VMEM((rows + POOL_HALO, d_pool), jnp.float32),
            pltpu.VMEM((rows, d_model), jnp.bfloat16),
        ],
        compiler_params=pltpu.CompilerParams(
            dimension_semantics=("arbitrary",),
            vmem_limit_bytes=_MIXER_VMEM_BYTES,
        ),
        name="token_mixer",
    )(x, *consts, w_up, w_down)


def _ffn_kernel(h2_ref, x1_hbm, wg_ref, wv_ref, cwg_ref, cwv_ref, bg_ref, bv_ref, wd_ref, gfin_ref,
                o_ref, gbuf, vbuf, gcarry, vcarry, x1buf, x1_sem):
    i = pl.program_id(0)
    j = pl.program_id(1)
    last_j = pl.num_programs(1) - 1
    rows = h2_ref.shape[0]

    def x1_copy():
        return pltpu.make_async_copy(x1_hbm.at[pl.ds(i * rows, rows), :], x1buf, x1_sem)

    @pl.when(i == 0)
    def _():
        gcarry[j] = jnp.zeros(gcarry.shape[1:], jnp.float32)
        vcarry[j] = jnp.zeros(vcarry.shape[1:], jnp.float32)

    @pl.when(j == 0)
    def _():
        x1_copy().start()
        o_ref[...] = jnp.zeros(o_ref.shape, jnp.float32)

    @pl.when(j == last_j)
    def _():
        x1_copy().wait()

    slabs = gbuf.shape[0]
    gbuf[:, 0:CONV_HALO, :] = gcarry[j]
    vbuf[:, 0:CONV_HALO, :] = vcarry[j]
    for w_ref, buf in ((wg_ref, gbuf), (wv_ref, vbuf)):
        for r in range(0, rows, FFN_SUB_ROWS):
            u = _dot(h2_ref[r:r + FFN_SUB_ROWS, :], w_ref[...])
            for c in range(slabs):
                buf[c, CONV_HALO + r:CONV_HALO + r + FFN_SUB_ROWS, :] = u[:, c * V7X_LANES:(c + 1) * V7X_LANES]
    for r in range(0, rows, FFN_SUB_ROWS):
        part = slice(r, r + FFN_SUB_ROWS)
        halo_part = slice(r, r + CONV_HALO + FFN_SUB_ROWS)
        act = []
        for c in range(slabs):
            lanes = slice(c * V7X_LANES, (c + 1) * V7X_LANES)
            gate = _causal_conv3(gbuf.at[c, halo_part, :], cwg_ref.at[:, lanes], FFN_SUB_ROWS) + bg_ref[:, lanes]
            val = _causal_conv3(vbuf.at[c, halo_part, :], cwv_ref.at[:, lanes], FFN_SUB_ROWS) + bv_ref[:, lanes]
            act.append((gate * jax.nn.sigmoid(gate) * val).astype(jnp.bfloat16))
        o_ref[part, :] += _dot(jnp.concatenate(act, axis=1), wd_ref[...])
    gcarry[j] = gbuf[:, rows:rows + CONV_HALO, :]
    vcarry[j] = vbuf[:, rows:rows + CONV_HALO, :]

    @pl.when(j == last_j)
    def _():
        def norm_rows(c, carry):
            r = pl.multiple_of(c * FFN_NORM_ROWS, FFN_NORM_ROWS)
            y = x1buf[pl.ds(r, FFN_NORM_ROWS), :] + o_ref[pl.ds(r, FFN_NORM_ROWS), :]
            o_ref[pl.ds(r, FFN_NORM_ROWS), :] = _rmsnorm(y, gfin_ref[...])
            return carry

        lax.fori_loop(0, rows // FFN_NORM_ROWS, norm_rows, None)


def _ffn(h2, x1, w_up, ffn_conv_w, ffn_conv_b, w_down, g_final):
    seq, d_model = x1.shape
    d_ff = w_down.shape[0]
    rows, cols = FFN_ROWS, FFN_COLS
    assert seq % rows == 0 and d_ff % cols == 0 and w_up.shape == (2 * d_ff // cols, d_model, cols)
    assert rows % FFN_SUB_ROWS == 0 and rows % FFN_NORM_ROWS == 0
    n_j = d_ff // cols
    row_spec = lambda: pl.BlockSpec((rows, d_model), lambda i, j: (i, 0))
    gate_cols = lambda r: pl.BlockSpec((r, cols), lambda i, j: (0, j))
    val_cols = lambda r: pl.BlockSpec((r, cols), lambda i, j: (0, j + n_j))
    up_gate = pl.BlockSpec((None, d_model, cols), lambda i, j: (j, 0, 0))
    up_val = pl.BlockSpec((None, d_model, cols), lambda i, j: (j + n_j, 0, 0))
    return pl.pallas_call(
        _ffn_kernel,
        grid=(seq // rows, n_j),
        in_specs=[
            row_spec(), pl.BlockSpec(memory_space=pl.ANY),
            up_gate, up_val,
            gate_cols(CONV_WIDTH), val_cols(CONV_WIDTH),
            gate_cols(1), val_cols(1),
            pl.BlockSpec((cols, d_model), lambda i, j: (j, 0)),
            pl.BlockSpec((1, d_model), lambda i, j: (0, 0)),
        ],
        out_specs=row_spec(),
        out_shape=jax.ShapeDtypeStruct((seq, d_model), jnp.float32),
        scratch_shapes=[
            pltpu.VMEM((cols // V7X_LANES, rows + CONV_HALO, V7X_LANES), jnp.float32),
            pltpu.VMEM((cols // V7X_LANES, rows + CONV_HALO, V7X_LANES), jnp.float32),
            pltpu.VMEM((n_j, cols // V7X_LANES, CONV_HALO, V7X_LANES), jnp.float32),
            pltpu.VMEM((n_j, cols // V7X_LANES, CONV_HALO, V7X_LANES), jnp.float32),
            pltpu.VMEM((rows, d_model), jnp.float32),
            pltpu.SemaphoreType.DMA(()),
        ],
        compiler_params=pltpu.CompilerParams(
            dimension_semantics=("arbitrary", "arbitrary"),
            vmem_limit_bytes=_FFN_VMEM_BYTES,
        ),
        name="ffn",
    )(h2, x1, w_up, w_up, ffn_conv_w, ffn_conv_w, ffn_conv_b, ffn_conv_b, w_down, g_final)


_MIXER_VMEM_BYTES = 56 * 1024 * 1024
_FFN_VMEM_BYTES = 60 * 1024 * 1024


def _block_diag(pool_w):
    groups, dim, _ = pool_w.shape
    bands = [jnp.pad(pool_w[g], ((0, 0), (g * dim, (groups - 1 - g) * dim))) for g in range(groups)]
    return jnp.concatenate(bands, axis=0)


def kernel(x, mem, g_mix, g_mem, w_in, conv_w, pool_w, pool_scale, w_kv, w_out, g_ffn, w_up,
           ffn_conv_w, ffn_conv_b, w_down, g_final):
    assert x.shape[0] == 1 and mem.shape[0] == 1 and w_in.shape[0] == 1
    assert tuple(sorted(POOL_WINDOWS)) == POOL_WINDOWS and pool_w.shape[1] == len(POOL_WINDOWS)
    bf16 = jnp.bfloat16
    d_conv = conv_w.shape[2]
    d_pool = pool_scale.shape[1]
    k, v = _kv_proj(mem[0], g_mem[0][None, :], w_kv[0])
    x1, h2, w_up16, w_down16 = _mixer(
        x[0], g_mix[0][None, :], w_in[0].astype(bf16), conv_w[0], _block_diag(pool_w[0]).astype(bf16),
        pool_scale[0][None, :], k, v, w_out[0].astype(bf16), g_ffn[0][None, :], w_up[0], w_down[0],
        d_conv=d_conv, d_pool=d_pool)
    out = _ffn(h2, x1, w_up16, ffn_conv_w[0], ffn_conv_b[0][None, :], w_down16, g_final[None, :])
    return out[None]
```

```python
import functools
import math

import jax
import jax.numpy as jnp
from jax import lax
from jax.experimental import pallas as pl
from jax.experimental.pallas import tpu as pltpu

EPS = 1e-6
POOL_WINDOWS = (2, 4, 8, 16)
XATT_HEADS = 4
CONV_WIDTH = 3

V7X_SUBLANES = 8
V7X_LANES = 128
BF16_SUBLANES = 2 * V7X_SUBLANES
V7X_VMEM_BYTES = 64 * 1024 * 1024

MIX_ROWS = 256
FFN_ROWS = 1024
FFN_SUB_ROWS = 512
FFN_NORM_ROWS = 128
FFN_COLS = 512
POOL_HALO = 16
CONV_HALO = V7X_SUBLANES


def _rmsnorm(x, g):
    return x * lax.rsqrt(jnp.mean(x * x, axis=-1, keepdims=True) + EPS) * g


def _dot(a, b):
    return jnp.dot(a, b, preferred_element_type=jnp.float32)


def _causal_conv3(buf_ref, w_ref, rows):
    out = w_ref[2:3, :] * buf_ref[CONV_HALO:CONV_HALO + rows, :]
    for j in range(CONV_WIDTH - 1):
        lag = CONV_WIDTH - 1 - j
        out = out + w_ref[j:j + 1, :] * buf_ref[CONV_HALO - lag:CONV_HALO - lag + rows, :]
    return out


def _kv_kernel(mem_ref, g_ref, w_ref, k_ref, v_ref):
    m = _rmsnorm(mem_ref[...], g_ref[...]).astype(jnp.bfloat16)
    kv = _dot(m, w_ref[...].astype(jnp.bfloat16))
    heads, _, head_dim = k_ref.shape
    for hd in range(heads):
        k_ref[hd] = kv[:, hd * head_dim:(hd + 1) * head_dim].astype(jnp.bfloat16)
        v_ref[hd] = kv[:, (heads + hd) * head_dim:(heads + hd + 1) * head_dim].astype(jnp.bfloat16)


def _kv_proj(mem, g_mem, w_kv):
    n_mem, _ = mem.shape
    head_dim = w_kv.shape[1] // (2 * XATT_HEADS)
    out = jax.ShapeDtypeStruct((XATT_HEADS, n_mem, head_dim), jnp.bfloat16)
    return pl.pallas_call(_kv_kernel, out_shape=(out, out), name="kv_proj")(mem, g_mem, w_kv)


def _mixer_kernel(x_ref, gmix_ref, win_ref, convw_ref, poolw_ref, pscale_ref, k_ref, v_ref,
                  wout_ref, gffn_ref, wup_ref, wdown_ref, x1_ref, h2_ref, wup16_ref, wdown16_ref,
                  zbuf, pbuf, vbuf, mixbuf, *, d_conv, d_pool, head_dim):
    i = pl.program_id(0)
    rows = x_ref.shape[0]

    @pl.when(i == 0)
    def _():
        pbuf[:, 0:CONV_HALO, :] = jnp.zeros((pbuf.shape[0], CONV_HALO, V7X_LANES), jnp.float32)
        vbuf[:, 0:POOL_HALO, :] = jnp.zeros((vbuf.shape[0], POOL_HALO, V7X_LANES), jnp.float32)

    x = x_ref[...]
    h = _rmsnorm(x, gmix_ref[...]).astype(jnp.bfloat16)
    zbuf[...] = _dot(h, win_ref[...])

    def cast_up_chunks(lo, hi):
        ff_cols = wup16_ref.shape[2]
        for c in range(lo, hi):
            wup16_ref[c] = wup_ref[:, c * ff_cols:(c + 1) * ff_cols].astype(jnp.bfloat16)

    up_chunks = wup16_ref.shape[0]
    cast_up_chunks(0, up_chunks // 3)

    c0 = d_conv
    for ci, c in enumerate(range(0, d_conv, V7X_LANES)):
        lanes = slice(c, c + V7X_LANES)
        cc, cx = zbuf[:, c0 + c:c0 + c + V7X_LANES], zbuf[:, 2 * c0 + c:2 * c0 + c + V7X_LANES]
        pbuf[ci, CONV_HALO:CONV_HALO + rows, :] = cc * cx
        conv = _causal_conv3(pbuf.at[ci], convw_ref.at[:, lanes], rows)
        mixbuf[:, lanes] = (zbuf[:, lanes] * conv).astype(jnp.bfloat16)
    pbuf[:, 0:CONV_HALO, :] = pbuf[:, rows:rows + CONV_HALO, :]

    cast_up_chunks(up_chunks // 3, 2 * up_chunks // 3)

    p0 = 3 * d_conv
    group_dim = d_pool // len(POOL_WINDOWS)
    t = (i * rows + lax.broadcasted_iota(jnp.int32, (rows, 1), 0)).astype(jnp.float32)
    for ci, c in enumerate(range(0, d_pool, V7X_LANES)):
        col = c + lax.broadcasted_iota(jnp.int32, (1, V7X_LANES), 1)
        cur = zbuf[:, p0 + c:p0 + c + V7X_LANES]
        vbuf[ci, POOL_HALO:POOL_HALO + rows, :] = cur
        acc = cur
        lag = 1
        pooled = None
        for g in range(c // group_dim, (c + V7X_LANES - 1) // group_dim + 1):
            k = POOL_WINDOWS[g]
            while lag < k:
                acc = acc + vbuf[ci, POOL_HALO - lag:POOL_HALO - lag + rows, :]
                lag += 1
            mean_k = acc / jnp.minimum(t + 1.0, float(k))
            pooled = mean_k if pooled is None else jnp.where(col >= g * group_dim, mean_k, pooled)
        mixbuf[:, c0 + c:c0 + c + V7X_LANES] = (pooled - cur).astype(jnp.bfloat16)
    vbuf[:, 0:POOL_HALO, :] = vbuf[:, rows:rows + POOL_HALO, :]
    pool_out = _dot(mixbuf[:, c0:c0 + d_pool], poolw_ref[...]) * pscale_ref[...]
    mixbuf[:, c0:c0 + d_pool] = pool_out.astype(jnp.bfloat16)

    cast_up_chunks(2 * up_chunks // 3, up_chunks)

    q0 = p0 + d_pool
    a0 = c0 + d_pool
    q = jnp.stack([zbuf[:, q0 + hd * head_dim:q0 + (hd + 1) * head_dim] for hd in range(XATT_HEADS)])
    s = jnp.einsum("hqd,hmd->hqm", q.astype(jnp.bfloat16), k_ref[...],
                   preferred_element_type=jnp.float32) * (1.0 / math.sqrt(head_dim))
    e = jnp.exp(s - jnp.max(s, axis=-1, keepdims=True))
    probs = (e / jnp.sum(e, axis=-1, keepdims=True)).astype(jnp.bfloat16)
    att = jnp.einsum("hqm,hmd->hqd", probs, v_ref[...], preferred_element_type=jnp.float32)
    for hd in range(XATT_HEADS):
        mixbuf[:, a0 + hd * head_dim:a0 + (hd + 1) * head_dim] = att[hd].astype(jnp.bfloat16)

    wdown16_ref[...] = wdown_ref[...].astype(jnp.bfloat16)
    x1 = x + _dot(mixbuf[...], wout_ref[...])
    x1_ref[...] = x1
    h2_ref[...] = _rmsnorm(x1, gffn_ref[...]).astype(jnp.bfloat16)


def _const_spec(shape):
    return pl.BlockSpec(shape, lambda i: (0,) * len(shape), pipeline_mode=pl.Buffered(1))


def _mixer(x, g_mix, w_in, conv_w, pool_bd, pool_scale, k, v, w_out, g_ffn, w_up, w_down, *, d_conv, d_pool):
    seq, d_model = x.shape
    rows = MIX_ROWS
    steps = seq // rows
    assert seq % rows == 0 and rows % POOL_HALO == 0
    head_dim = k.shape[2]
    row_spec = pl.BlockSpec((rows, d_model), lambda i: (i, 0))
    consts = (g_mix, w_in, conv_w, pool_bd, pool_scale, k, v, w_out, g_ffn)
    d_ff = w_down.shape[0]
    up_chunks = w_up.shape[1] // FFN_COLS
    up_rows = d_model // steps
    down_rows = 2 * d_ff // steps
    assert d_model % steps == 0 and (2 * d_ff) % steps == 0 and steps % 2 == 0
    assert up_rows % BF16_SUBLANES == 0 and down_rows % BF16_SUBLANES == 0 and w_up.shape[1] % FFN_COLS == 0
    return pl.pallas_call(
        functools.partial(_mixer_kernel, d_conv=d_conv, d_pool=d_pool, head_dim=head_dim),
        grid=(steps,),
        in_specs=[row_spec] + [_const_spec(c.shape) for c in consts] + [
            pl.BlockSpec((up_rows, w_up.shape[1]), lambda i: (i, 0)),
            pl.BlockSpec((down_rows, d_model), lambda i: (i // 2, 0)),
        ],
        out_specs=(row_spec, row_spec,
                   pl.BlockSpec((up_chunks, up_rows, FFN_COLS), lambda i: (0, i, 0)),
                   pl.BlockSpec((down_rows, d_model), lambda i: (i // 2, 0))),
        out_shape=(jax.ShapeDtypeStruct((seq, d_model), jnp.float32),
                   jax.ShapeDtypeStruct((seq, d_model), jnp.bfloat16),
                   jax.ShapeDtypeStruct((up_chunks, d_model, FFN_COLS), jnp.bfloat16),
                   jax.ShapeDtypeStruct((d_ff, d_model), jnp.bfloat16)),
        scratch_shapes=[
            pltpu.VMEM((rows, w_in.shape[1]), jnp.float32),
            pltpu.VMEM((d_conv // V7X_LANES, rows + CONV_HALO, V7X_LANES), jnp.float32),
            pltpu.VMEM((d_pool // V7X_LANES, rows + POOL_HALO, V7X_LANES), jnp.float32),
            pltpu.VMEM((rows, d_model), jnp.bfloat16),
        ],
        compiler_params=pltpu.CompilerParams(
            dimension_semantics=("arbitrary",),
            vmem_limit_bytes=_MIXER_VMEM_BYTES,
        ),
        name="token_mixer",
    )(x, *consts, w_up, w_down)


def _ffn_kernel(h2_ref, x1_hbm, wg_ref, wv_ref, cwg_ref, cwv_ref, bg_ref, bv_ref, wd_ref, gfin_ref,
                o_ref, gbuf, vbuf, gcarry, vcarry, x1buf, x1_sem):
    i = pl.program_id(0)
    j = pl.program_id(1)
    last_j = pl.num_programs(1) - 1
    rows = h2_ref.shape[0]

    def x1_copy():
        return pltpu.make_async_copy(x1_hbm.at[pl.ds(i * rows, rows), :], x1buf, x1_sem)

    @pl.when(i == 0)
    def _():
        gcarry[j] = jnp.zeros(gcarry.shape[1:], jnp.float32)
        vcarry[j] = jnp.zeros(vcarry.shape[1:], jnp.float32)

    @pl.when(j == 0)
    def _():
        x1_copy().start()
        o_ref[...] = jnp.zeros(o_ref.shape, jnp.float32)

    @pl.when(j == last_j)
    def _():
        x1_copy().wait()

    slabs = gbuf.shape[0]
    gbuf[:, 0:CONV_HALO, :] = gcarry[j]
    vbuf[:, 0:CONV_HALO, :] = vcarry[j]
    for w_ref, buf in ((wg_ref, gbuf), (wv_ref, vbuf)):
        for r in range(0, rows, FFN_SUB_ROWS):
            u = _dot(h2_ref[r:r + FFN_SUB_ROWS, :], w_ref[...])
            for c in range(slabs):
                buf[c, CONV_HALO + r:CONV_HALO + r + FFN_SUB_ROWS, :] = u[:, c * V7X_LANES:(c + 1) * V7X_LANES]
    for r in range(0, rows, FFN_SUB_ROWS):
        part = slice(r, r + FFN_SUB_ROWS)
        halo_part = slice(r, r + CONV_HALO + FFN_SUB_ROWS)
        act = []
        for c in range(slabs):
            lanes = slice(c * V7X_LANES, (c + 1) * V7X_LANES)
            gate = _causal_conv3(gbuf.at[c, halo_part, :], cwg_ref.at[:, lanes], FFN_SUB_ROWS) + bg_ref[:, lanes]
            val = _causal_conv3(vbuf.at[c, halo_part, :], cwv_ref.at[:, lanes], FFN_SUB_ROWS) + bv_ref[:, lanes]
            act.append((gate * jax.nn.sigmoid(gate) * val).astype(jnp.bfloat16))
        o_ref[part, :] += _dot(jnp.concatenate(act, axis=1), wd_ref[...])
    gcarry[j] = gbuf[:, rows:rows + CONV_HALO, :]
    vcarry[j] = vbuf[:, rows:rows + CONV_HALO, :]

    @pl.when(j == last_j)
    def _():
        def norm_rows(c, carry):
            r = pl.multiple_of(c * FFN_NORM_ROWS, FFN_NORM_ROWS)
            y = x1buf[pl.ds(r, FFN_NORM_ROWS), :] + o_ref[pl.ds(r, FFN_NORM_ROWS), :]
            o_ref[pl.ds(r, FFN_NORM_ROWS), :] = _rmsnorm(y, gfin_ref[...])
            return carry

        lax.fori_loop(0, rows // FFN_NORM_ROWS, norm_rows, None)


def _ffn(h2, x1, w_up, ffn_conv_w, ffn_conv_b, w_down, g_final):
    seq, d_model = x1.shape
    d_ff = w_down.shape[0]
    rows, cols = FFN_ROWS, FFN_COLS
    assert seq % rows == 0 and d_ff % cols == 0 and w_up.shape == (2 * d_ff // cols, d_model, cols)
    assert rows % FFN_SUB_ROWS == 0 and rows % FFN_NORM_ROWS == 0
    n_j = d_ff // cols
    row_spec = lambda: pl.BlockSpec((rows, d_model), lambda i, j: (i, 0))
    gate_cols = lambda r: pl.BlockSpec((r, cols), lambda i, j: (0, j))
    val_cols = lambda r: pl.BlockSpec((r, cols), lambda i, j: (0, j + n_j))
    up_gate = pl.BlockSpec((None, d_model, cols), lambda i, j: (j, 0, 0))
    up_val = pl.BlockSpec((None, d_model, cols), lambda i, j: (j + n_j, 0, 0))
    return pl.pallas_call(
        _ffn_kernel,
        grid=(seq // rows, n_j),
        in_specs=[
            row_spec(), pl.BlockSpec(memory_space=pl.ANY),
            up_gate, up_val,
            gate_cols(CONV_WIDTH), val_cols(CONV_WIDTH),
            gate_cols(1), val_cols(1),
            pl.BlockSpec((cols, d_model), lambda i, j: (j, 0)),
            pl.BlockSpec((1, d_model), lambda i, j: (0, 0)),
        ],
        out_specs=row_spec(),
        out_shape=jax.ShapeDtypeStruct((seq, d_model), jnp.float32),
        scratch_shapes=[
            pltpu.VMEM((cols // V7X_LANES, rows + CONV_HALO, V7X_LANES), jnp.float32),
            pltpu.VMEM((cols // V7X_LANES, rows + CONV_HALO, V7X_LANES), jnp.float32),
            pltpu.VMEM((n_j, cols // V7X_LANES, CONV_HALO, V7X_LANES), jnp.float32),
            pltpu.VMEM((n_j, cols // V7X_LANES, CONV_HALO, V7X_LANES), jnp.float32),
            pltpu.VMEM((rows, d_model), jnp.float32),
            pltpu.SemaphoreType.DMA(()),
        ],
        compiler_params=pltpu.CompilerParams(
            dimension_semantics=("arbitrary", "arbitrary"),
            vmem_limit_bytes=_FFN_VMEM_BYTES,
        ),
        name="ffn",
    )(h2, x1, w_up, w_up, ffn_conv_w, ffn_conv_w, ffn_conv_b, ffn_conv_b, w_down, g_final)


_MIXER_VMEM_BYTES = 56 * 1024 * 1024
_FFN_VMEM_BYTES = 60 * 1024 * 1024


def _block_diag(pool_w):
    groups, dim, _ = pool_w.shape
    bands = [jnp.pad(pool_w[g], ((0, 0), (g * dim, (groups - 1 - g) * dim))) for g in range(groups)]
    return jnp.concatenate(bands, axis=0)


def kernel(x, mem, g_mix, g_mem, w_in, conv_w, pool_w, pool_scale, w_kv, w_out, g_ffn, w_up,
           ffn_conv_w, ffn_conv_b, w_down, g_final):
    assert x.shape[0] == 1 and mem.shape[0] == 1 and w_in.shape[0] == 1
    assert tuple(sorted(POOL_WINDOWS)) == POOL_WINDOWS and pool_w.shape[1] == len(POOL_WINDOWS)
    bf16 = jnp.bfloat16
    d_conv = conv_w.shape[2]
    d_pool = pool_scale.shape[1]
    k, v = _kv_proj(mem[0], g_mem[0][None, :], w_kv[0])
    x1, h2, w_up16, w_down16 = _mixer(
        x[0], g_mix[0][None, :], w_in[0].astype(bf16), conv_w[0], _block_diag(pool_w[0]).astype(bf16),
        pool_scale[0][None, :], k, v, w_out[0].astype(bf16), g_ffn[0][None, :], w_up[0], w_down[0],
        d_conv=d_conv, d_pool=d_pool)
    out = _ffn(h2, x1, w_up16, ffn_conv_w[0], ffn_conv_b[0][None, :], w_down16, g_final[None, :])
    return out[None]
```

```python
import functools
import math

import jax
import jax.numpy as jnp
from jax import lax
from jax.experimental import pallas as pl
from jax.experimental.pallas import tpu as pltpu

EPS = 1e-6
POOL_WINDOWS = (2, 4, 8, 16)
XATT_HEADS = 4
CONV_WIDTH = 3

V7X_SUBLANES = 8
V7X_LANES = 128
BF16_SUBLANES = 2 * V7X_SUBLANES
V7X_VMEM_BYTES = 64 * 1024 * 1024

MIX_ROWS = 256
FFN_ROWS = 1024
FFN_SUB_ROWS = 512
FFN_NORM_ROWS = 128
FFN_COLS = 512
POOL_HALO = 16
CONV_HALO = V7X_SUBLANES


def _rmsnorm(x, g):
    return x * lax.rsqrt(jnp.mean(x * x, axis=-1, keepdims=True) + EPS) * g


def _dot(a, b):
    return jnp.dot(a, b, preferred_element_type=jnp.float32)


def _causal_conv3(buf_ref, w_ref, rows):
    out = w_ref[2:3, :] * buf_ref[CONV_HALO:CONV_HALO + rows, :]
    for j in range(CONV_WIDTH - 1):
        lag = CONV_WIDTH - 1 - j
        out = out + w_ref[j:j + 1, :] * buf_ref[CONV_HALO - lag:CONV_HALO - lag + rows, :]
    return out


def _kv_kernel(mem_ref, g_ref, w_ref, k_ref, v_ref):
    m = _rmsnorm(mem_ref[...], g_ref[...]).astype(jnp.bfloat16)
    kv = _dot(m, w_ref[...].astype(jnp.bfloat16))
    heads, _, head_dim = k_ref.shape
    for hd in range(heads):
        k_ref[hd] = kv[:, hd * head_dim:(hd + 1) * head_dim].astype(jnp.bfloat16)
        v_ref[hd] = kv[:, (heads + hd) * head_dim:(heads + hd + 1) * head_dim].astype(jnp.bfloat16)


def _kv_proj(mem, g_mem, w_kv):
    n_mem, _ = mem.shape
    head_dim = w_kv.shape[1] // (2 * XATT_HEADS)
    out = jax.ShapeDtypeStruct((XATT_HEADS, n_mem, head_dim), jnp.bfloat16)
    return pl.pallas_call(_kv_kernel, out_shape=(out, out), name="kv_proj")(mem, g_mem, w_kv)


def _mixer_kernel(x_ref, gmix_ref, win_ref, convw_ref, poolw_ref, pscale_ref, k_ref, v_ref,
                  wout_ref, gffn_ref, wup_ref, wdown_ref, x1_ref, h2_ref, wup16_ref, wdown16_ref,
                  zbuf, pbuf, vbuf, sbuf, mixbuf, *, d_conv, d_pool, head_dim):
    i = pl.program_id(0)
    rows = x_ref.shape[0]

    @pl.when(i == 0)
    def _():
        pbuf[0:CONV_HALO, :] = jnp.zeros((CONV_HALO, d_conv), jnp.float32)
        vbuf[0:POOL_HALO, :] = jnp.zeros((POOL_HALO, d_pool), jnp.float32)
        sbuf[:, :, 0:POOL_HALO, :] = jnp.zeros(sbuf.shape[:2] + (POOL_HALO, V7X_LANES), jnp.float32)

    x = x_ref[...]
    h = _rmsnorm(x, gmix_ref[...]).astype(jnp.bfloat16)
    zbuf[...] = _dot(h, win_ref[...])

    def cast_up_chunks(lo, hi):
        ff_cols = wup16_ref.shape[2]
        for c in range(lo, hi):
            wup16_ref[c] = wup_ref[:, c * ff_cols:(c + 1) * ff_cols].astype(jnp.bfloat16)

    up_chunks = wup16_ref.shape[0]
    cast_up_chunks(0, up_chunks // 3)

    c0 = d_conv
    pbuf[CONV_HALO:CONV_HALO + rows, :] = zbuf[:, c0:2 * c0] * zbuf[:, 2 * c0:3 * c0]
    conv = _causal_conv3(pbuf, convw_ref, rows)
    pbuf[0:CONV_HALO, :] = pbuf[rows:rows + CONV_HALO, :]
    mixbuf[:, 0:c0] = (zbuf[:, 0:c0] * conv).astype(jnp.bfloat16)

    cast_up_chunks(up_chunks // 3, 2 * up_chunks // 3)

    p0 = 3 * d_conv
    vbuf[POOL_HALO:POOL_HALO + rows, :] = zbuf[:, p0:p0 + d_pool]
    group_dim = d_pool // len(POOL_WINDOWS)
    t = (i * rows + lax.broadcasted_iota(jnp.int32, (rows, 1), 0)).astype(jnp.float32)
    ext = rows + POOL_HALO
    for ci, c in enumerate(range(0, d_pool, V7X_LANES)):
        col = c + lax.broadcasted_iota(jnp.int32, (1, V7X_LANES), 1)
        groups = range(c // group_dim, (c + V7X_LANES - 1) // group_dim + 1)
        sums = {1: vbuf[0:ext, c:c + V7X_LANES]}
        k = 1
        while k < POOL_WINDOWS[groups[-1]]:
            stage = sbuf.at[ci, k.bit_length() - 1]
            stage[POOL_HALO:POOL_HALO + ext, :] = sums[k]
            sums[2 * k] = sums[k] + stage[POOL_HALO - k:POOL_HALO - k + ext, :]
            k *= 2
        cur = sums[1][POOL_HALO:, :]
        pooled = None
        for g in groups:
            k = POOL_WINDOWS[g]
            mean_k = sums[k][POOL_HALO:, :] / jnp.minimum(t + 1.0, float(k))
            pooled = mean_k if pooled is None else jnp.where(col >= g * group_dim, mean_k, pooled)
        mixbuf[:, c0 + c:c0 + c + V7X_LANES] = (pooled - cur).astype(jnp.bfloat16)
    vbuf[0:POOL_HALO, :] = vbuf[rows:rows + POOL_HALO, :]
    pool_out = _dot(mixbuf[:, c0:c0 + d_pool], poolw_ref[...]) * pscale_ref[...]
    mixbuf[:, c0:c0 + d_pool] = pool_out.astype(jnp.bfloat16)

    cast_up_chunks(2 * up_chunks // 3, up_chunks)

    q0 = p0 + d_pool
    a0 = c0 + d_pool
    q = jnp.stack([zbuf[:, q0 + hd * head_dim:q0 + (hd + 1) * head_dim] for hd in range(XATT_HEADS)])
    s = jnp.einsum("hqd,hmd->hqm", q.astype(jnp.bfloat16), k_ref[...],
                   preferred_element_type=jnp.float32) * (1.0 / math.sqrt(head_dim))
    e = jnp.exp(s - jnp.max(s, axis=-1, keepdims=True))
    probs = (e / jnp.sum(e, axis=-1, keepdims=True)).astype(jnp.bfloat16)
    att = jnp.einsum("hqm,hmd->hqd", probs, v_ref[...], preferred_element_type=jnp.float32)
    for hd in range(XATT_HEADS):
        mixbuf[:, a0 + hd * head_dim:a0 + (hd + 1) * head_dim] = att[hd].astype(jnp.bfloat16)

    wdown16_ref[...] = wdown_ref[...].astype(jnp.bfloat16)
    x1 = x + _dot(mixbuf[...], wout_ref[...])
    x1_ref[...] = x1
    h2_ref[...] = _rmsnorm(x1, gffn_ref[...]).astype(jnp.bfloat16)


def _const_spec(shape):
    return pl.BlockSpec(shape, lambda i: (0,) * len(shape), pipeline_mode=pl.Buffered(1))


def _mixer(x, g_mix, w_in, conv_w, pool_bd, pool_scale, k, v, w_out, g_ffn, w_up, w_down, *, d_conv, d_pool):
    seq, d_model = x.shape
    rows = MIX_ROWS
    steps = seq // rows
    assert seq % rows == 0 and rows % POOL_HALO == 0
    head_dim = k.shape[2]
    row_spec = pl.BlockSpec((rows, d_model), lambda i: (i, 0))
    consts = (g_mix, w_in, conv_w, pool_bd, pool_scale, k, v, w_out, g_ffn)
    d_ff = w_down.shape[0]
    up_chunks = w_up.shape[1] // FFN_COLS
    up_rows = d_model // steps
    down_rows = 2 * d_ff // steps
    assert d_model % steps == 0 and (2 * d_ff) % steps == 0 and steps % 2 == 0
    assert up_rows % BF16_SUBLANES == 0 and down_rows % BF16_SUBLANES == 0 and w_up.shape[1] % FFN_COLS == 0
    return pl.pallas_call(
        functools.partial(_mixer_kernel, d_conv=d_conv, d_pool=d_pool, head_dim=head_dim),
        grid=(steps,),
        in_specs=[row_spec] + [_const_spec(c.shape) for c in consts] + [
            pl.BlockSpec((up_rows, w_up.shape[1]), lambda i: (i, 0)),
            pl.BlockSpec((down_rows, d_model), lambda i: (i // 2, 0)),
        ],
        out_specs=(row_spec, row_spec,
                   pl.BlockSpec((up_chunks, up_rows, FFN_COLS), lambda i: (0, i, 0)),
                   pl.BlockSpec((down_rows, d_model), lambda i: (i // 2, 0))),
        out_shape=(jax.ShapeDtypeStruct((seq, d_model), jnp.float32),
                   jax.ShapeDtypeStruct((seq, d_model), jnp.bfloat16),
                   jax.ShapeDtypeStruct((up_chunks, d_model, FFN_COLS), jnp.bfloat16),
                   jax.ShapeDtypeStruct((d_ff, d_model), jnp.bfloat16)),
        scratch_shapes=[
            pltpu.VMEM((rows, w_in.shape[1]), jnp.float32),
            pltpu.VMEM((rows + CONV_HALO, d_conv), jnp.float32),
            pltpu.VMEM((rows + POOL_HALO, d_pool), jnp.float32),
            pltpu.VMEM((d_pool // V7X_LANES, len(POOL_WINDOWS), rows + 2 * POOL_HALO, V7X_LANES), jnp.float32),
            pltpu.VMEM((rows, d_model), jnp.bfloat16),
        ],
        compiler_params=pltpu.CompilerParams(
            dimension_semantics=("arbitrary",),
            vmem_limit_bytes=_MIXER_VMEM_BYTES,
        ),
        name="token_mixer",
    )(x, *consts, w_up, w_down)


def _ffn_kernel(h2_ref, x1_hbm, wg_ref, wv_ref, cwg_ref, cwv_ref, bg_ref, bv_ref, wd_ref, gfin_ref,
                o_ref, gbuf, vbuf, gcarry, vcarry, x1buf, x1_sem):
    i = pl.program_id(0)
    j = pl.program_id(1)
    last_j = pl.num_programs(1) - 1
    rows = h2_ref.shape[0]

    def x1_copy():
        return pltpu.make_async_copy(x1_hbm.at[pl.ds(i * rows, rows), :], x1buf, x1_sem)

    @pl.when(i == 0)
    def _():
        gcarry[j] = jnp.zeros(gcarry.shape[1:], jnp.float32)
        vcarry[j] = jnp.zeros(vcarry.shape[1:], jnp.float32)

    @pl.when(j == 0)
    def _():
        x1_copy().start()
        o_ref[...] = jnp.zeros(o_ref.shape, jnp.float32)

    @pl.when(j == last_j)
    def _():
        x1_copy().wait()

    slabs = gbuf.shape[0]
    gbuf[:, 0:CONV_HALO, :] = gcarry[j]
    vbuf[:, 0:CONV_HALO, :] = vcarry[j]
    for w_ref, buf in ((wg_ref, gbuf), (wv_ref, vbuf)):
        for r in range(0, rows, FFN_SUB_ROWS):
            u = _dot(h2_ref[r:r + FFN_SUB_ROWS, :], w_ref[...])
            for c in range(slabs):
                buf[c, CONV_HALO + r:CONV_HALO + r + FFN_SUB_ROWS, :] = u[:, c * V7X_LANES:(c + 1) * V7X_LANES]
    for r in range(0, rows, FFN_SUB_ROWS):
        part = slice(r, r + FFN_SUB_ROWS)
        halo_part = slice(r, r + CONV_HALO + FFN_SUB_ROWS)
        act = []
        for c in range(slabs):
            lanes = slice(c * V7X_LANES, (c + 1) * V7X_LANES)
            gate = _causal_conv3(gbuf.at[c, halo_part, :], cwg_ref.at[:, lanes], FFN_SUB_ROWS) + bg_ref[:, lanes]
            val = _causal_conv3(vbuf.at[c, halo_part, :], cwv_ref.at[:, lanes], FFN_SUB_ROWS) + bv_ref[:, lanes]
            act.append((gate * jax.nn.sigmoid(gate) * val).astype(jnp.bfloat16))
        o_ref[part, :] += _dot(jnp.concatenate(act, axis=1), wd_ref[...])
    gcarry[j] = gbuf[:, rows:rows + CONV_HALO, :]
    vcarry[j] = vbuf[:, rows:rows + CONV_HALO, :]

    @pl.when(j == last_j)
    def _():
        def norm_rows(c, carry):
            r = pl.multiple_of(c * FFN_NORM_ROWS, FFN_NORM_ROWS)
            y = x1buf[pl.ds(r, FFN_NORM_ROWS), :] + o_ref[pl.ds(r, FFN_NORM_ROWS), :]
            o_ref[pl.ds(r, FFN_NORM_ROWS), :] = _rmsnorm(y, gfin_ref[...])
            return carry

        lax.fori_loop(0, rows // FFN_NORM_ROWS, norm_rows, None)


def _ffn(h2, x1, w_up, ffn_conv_w, ffn_conv_b, w_down, g_final):
    seq, d_model = x1.shape
    d_ff = w_down.shape[0]
    rows, cols = FFN_ROWS, FFN_COLS
    assert seq % rows == 0 and d_ff % cols == 0 and w_up.shape == (2 * d_ff // cols, d_model, cols)
    assert rows % FFN_SUB_ROWS == 0 and rows % FFN_NORM_ROWS == 0
    n_j = d_ff // cols
    row_spec = lambda: pl.BlockSpec((rows, d_model), lambda i, j: (i, 0))
    gate_cols = lambda r: pl.BlockSpec((r, cols), lambda i, j: (0, j))
    val_cols = lambda r: pl.BlockSpec((r, cols), lambda i, j: (0, j + n_j))
    up_gate = pl.BlockSpec((None, d_model, cols), lambda i, j: (j, 0, 0))
    up_val = pl.BlockSpec((None, d_model, cols), lambda i, j: (j + n_j, 0, 0))
    return pl.pallas_call(
        _ffn_kernel,
        grid=(seq // rows, n_j),
        in_specs=[
            row_spec(), pl.BlockSpec(memory_space=pl.ANY),
            up_gate, up_val,
            gate_cols(CONV_WIDTH), val_cols(CONV_WIDTH),
            gate_cols(1), val_cols(1),
            pl.BlockSpec((cols, d_model), lambda i, j: (j, 0)),
            pl.BlockSpec((1, d_model), lambda i, j: (0, 0)),
        ],
        out_specs=row_spec(),
        out_shape=jax.ShapeDtypeStruct((seq, d_model), jnp.float32),
        scratch_shapes=[
            pltpu.VMEM((cols // V7X_LANES, rows + CONV_HALO, V7X_LANES), jnp.float32),
            pltpu.VMEM((cols // V7X_LANES, rows + CONV_HALO, V7X_LANES), jnp.float32),
            pltpu.VMEM((n_j, cols // V7X_LANES, CONV_HALO, V7X_LANES), jnp.float32),
            pltpu.VMEM((n_j, cols // V7X_LANES, CONV_HALO, V7X_LANES), jnp.float32),
            pltpu.VMEM((rows, d_model), jnp.float32),
            pltpu.SemaphoreType.DMA(()),
        ],
        compiler_params=pltpu.CompilerParams(
            dimension_semantics=("arbitrary", "arbitrary"),
            vmem_limit_bytes=_FFN_VMEM_BYTES,
        ),
        name="ffn",
    )(h2, x1, w_up, w_up, ffn_conv_w, ffn_conv_w, ffn_conv_b, ffn_conv_b, w_down, g_final)


_MIXER_VMEM_BYTES = 56 * 1024 * 1024
_FFN_VMEM_BYTES = 60 * 1024 * 1024


def _block_diag(pool_w):
    groups, dim, _ = pool_w.shape
    bands = [jnp.pad(pool_w[g], ((0, 0), (g * dim, (groups - 1 - g) * dim))) for g in range(groups)]
    return jnp.concatenate(bands, axis=0)


def kernel(x, mem, g_mix, g_mem, w_in, conv_w, pool_w, pool_scale, w_kv, w_out, g_ffn, w_up,
           ffn_conv_w, ffn_conv_b, w_down, g_final):
    assert x.shape[0] == 1 and mem.shape[0] == 1 and w_in.shape[0] == 1
    assert POOL_WINDOWS == tuple(2 ** (n + 1) for n in range(len(POOL_WINDOWS))) and pool_w.shape[1] == len(POOL_WINDOWS)
    bf16 = jnp.bfloat16
    d_conv = conv_w.shape[2]
    d_pool = pool_scale.shape[1]
    k, v = _kv_proj(mem[0], g_mem[0][None, :], w_kv[0])
    x1, h2, w_up16, w_down16 = _mixer(
        x[0], g_mix[0][None, :], w_in[0].astype(bf16), conv_w[0], _block_diag(pool_w[0]).astype(bf16),
        pool_scale[0][None, :], k, v, w_out[0].astype(bf16), g_ffn[0][None, :], w_up[0], w_down[0],
        d_conv=d_conv, d_pool=d_pool)
    out = _ffn(h2, x1, w_up16, ffn_conv_w[0], ffn_conv_b[0][None, :], w_down16, g_final[None, :])
    return out[None]
```

```python
import functools
import math

import jax
import jax.numpy as jnp
from jax import lax
from jax.experimental import pallas as pl
from jax.experimental.pallas import tpu as pltpu

EPS = 1e-6
POOL_WINDOWS = (2, 4, 8, 16)
XATT_HEADS = 4
CONV_WIDTH = 3

V7X_SUBLANES = 8
V7X_LANES = 128
BF16_SUBLANES = 2 * V7X_SUBLANES
V7X_VMEM_BYTES = 64 * 1024 * 1024

MIX_ROWS = 256
MIX_STAGE_ROWS = 128
FFN_ROWS = 1024
FFN_SUB_ROWS = 512
FFN_NORM_ROWS = 128
FFN_COLS = 512
POOL_HALO = 16
CONV_HALO = V7X_SUBLANES


def _rmsnorm(x, g):
    return x * lax.rsqrt(jnp.mean(x * x, axis=-1, keepdims=True) + EPS) * g


def _dot(a, b):
    return jnp.dot(a, b, preferred_element_type=jnp.float32)


def _causal_conv3(buf_ref, w_ref, rows):
    out = w_ref[2:3, :] * buf_ref[CONV_HALO:CONV_HALO + rows, :]
    for j in range(CONV_WIDTH - 1):
        lag = CONV_WIDTH - 1 - j
        out = out + w_ref[j:j + 1, :] * buf_ref[CONV_HALO - lag:CONV_HALO - lag + rows, :]
    return out


def _kv_kernel(mem_ref, g_ref, w_ref, k_ref, v_ref):
    m = _rmsnorm(mem_ref[...], g_ref[...]).astype(jnp.bfloat16)
    kv = _dot(m, w_ref[...].astype(jnp.bfloat16))
    heads, _, head_dim = k_ref.shape
    for hd in range(heads):
        k_ref[hd] = kv[:, hd * head_dim:(hd + 1) * head_dim].astype(jnp.bfloat16)
        v_ref[hd] = kv[:, (heads + hd) * head_dim:(heads + hd + 1) * head_dim].astype(jnp.bfloat16)


def _kv_proj(mem, g_mem, w_kv):
    n_mem, _ = mem.shape
    head_dim = w_kv.shape[1] // (2 * XATT_HEADS)
    out = jax.ShapeDtypeStruct((XATT_HEADS, n_mem, head_dim), jnp.bfloat16)
    return pl.pallas_call(_kv_kernel, out_shape=(out, out), name="kv_proj")(mem, g_mem, w_kv)


def _load_as_bf16(w_hbm, w16, stage, sems):
    chunk = stage.shape[1]
    n_chunks = w_hbm.shape[0] // chunk
    cols = w_hbm.shape[1]

    def copy(c, slot):
        return pltpu.make_async_copy(w_hbm.at[pl.ds(pl.multiple_of(c * chunk, chunk), chunk), :],
                                     stage.at[slot, :, pl.ds(0, cols)], sems.at[slot])

    copy(0, 0).start()

    def step(c, carry):
        slot = c % 2

        @pl.when(c + 1 < n_chunks)
        def _():
            copy(c + 1, 1 - slot).start()

        copy(c, slot).wait()
        w16[pl.ds(pl.multiple_of(c * chunk, chunk), chunk), :] = stage[slot, :, 0:cols].astype(jnp.bfloat16)
        return carry

    lax.fori_loop(0, n_chunks, step, None)


def _mixer_kernel(x_ref, gmix_ref, convw_ref, poolw_ref, pscale_ref, k_ref, v_ref, gffn_ref, win_hbm, wout_hbm,
                  wup_ref, wdown_ref, x1_ref, h2_ref, wup16_ref, wdown16_ref,
                  win_ref, wout_ref, stage, stage_sems, zbuf, pbuf, vbuf, mixbuf, *, d_conv, d_pool, head_dim):
    i = pl.program_id(0)
    rows = x_ref.shape[0]

    @pl.when(i == 0)
    def _():
        pbuf[0:CONV_HALO, :] = jnp.zeros((CONV_HALO, d_conv), jnp.float32)
        vbuf[0:POOL_HALO, :] = jnp.zeros((POOL_HALO, d_pool), jnp.float32)
        _load_as_bf16(win_hbm, win_ref, stage, stage_sems)
        _load_as_bf16(wout_hbm, wout_ref, stage, stage_sems)

    x = x_ref[...]
    h = _rmsnorm(x, gmix_ref[...]).astype(jnp.bfloat16)
    zbuf[...] = _dot(h, win_ref[...])

    def cast_up_chunks(lo, hi):
        ff_cols = wup16_ref.shape[2]
        for c in range(lo, hi):
            wup16_ref[c] = wup_ref[:, c * ff_cols:(c + 1) * ff_cols].astype(jnp.bfloat16)

    up_chunks = wup16_ref.shape[0]
    cast_up_chunks(0, up_chunks // 3)

    c0 = d_conv
    pbuf[CONV_HALO:CONV_HALO + rows, :] = zbuf[:, c0:2 * c0] * zbuf[:, 2 * c0:3 * c0]
    conv = _causal_conv3(pbuf, convw_ref, rows)
    pbuf[0:CONV_HALO, :] = pbuf[rows:rows + CONV_HALO, :]
    mixbuf[:, 0:c0] = (zbuf[:, 0:c0] * conv).astype(jnp.bfloat16)

    cast_up_chunks(up_chunks // 3, 2 * up_chunks // 3)

    p0 = 3 * d_conv
    vbuf[POOL_HALO:POOL_HALO + rows, :] = zbuf[:, p0:p0 + d_pool]
    group_dim = d_pool // len(POOL_WINDOWS)
    t = (i * rows + lax.broadcasted_iota(jnp.int32, (rows, 1), 0)).astype(jnp.float32)
    for c in range(0, d_pool, V7X_LANES):
        col = c + lax.broadcasted_iota(jnp.int32, (1, V7X_LANES), 1)
        cur = vbuf[POOL_HALO:POOL_HALO + rows, c:c + V7X_LANES]
        acc = cur
        lag = 1
        pooled = None
        for g in range(c // group_dim, (c + V7X_LANES - 1) // group_dim + 1):
            k = POOL_WINDOWS[g]
            while lag < k:
                acc = acc + vbuf[POOL_HALO - lag:POOL_HALO - lag + rows, c:c + V7X_LANES]
                lag += 1
            mean_k = acc / jnp.minimum(t + 1.0, float(k))
            pooled = mean_k if pooled is None else jnp.where(col >= g * group_dim, mean_k, pooled)
        mixbuf[:, c0 + c:c0 + c + V7X_LANES] = (pooled - cur).astype(jnp.bfloat16)
    vbuf[0:POOL_HALO, :] = vbuf[rows:rows + POOL_HALO, :]
    pool_out = _dot(mixbuf[:, c0:c0 + d_pool], poolw_ref[...]) * pscale_ref[...]
    mixbuf[:, c0:c0 + d_pool] = pool_out.astype(jnp.bfloat16)

    cast_up_chunks(2 * up_chunks // 3, up_chunks)

    q0 = p0 + d_pool
    a0 = c0 + d_pool
    q = jnp.stack([zbuf[:, q0 + hd * head_dim:q0 + (hd + 1) * head_dim] for hd in range(XATT_HEADS)])
    s = jnp.einsum("hqd,hmd->hqm", q.astype(jnp.bfloat16), k_ref[...],
                   preferred_element_type=jnp.float32) * (1.0 / math.sqrt(head_dim))
    e = jnp.exp(s - jnp.max(s, axis=-1, keepdims=True))
    probs = (e / jnp.sum(e, axis=-1, keepdims=True)).astype(jnp.bfloat16)
    att = jnp.einsum("hqm,hmd->hqd", probs, v_ref[...], preferred_element_type=jnp.float32)
    for hd in range(XATT_HEADS):
        mixbuf[:, a0 + hd * head_dim:a0 + (hd + 1) * head_dim] = att[hd].astype(jnp.bfloat16)

    wdown16_ref[...] = wdown_ref[...].astype(jnp.bfloat16)
    x1 = x + _dot(mixbuf[...], wout_ref[...])
    x1_ref[...] = x1
    h2_ref[...] = pltpu.bitcast(_rmsnorm(x1, gffn_ref[...]).astype(jnp.bfloat16), jnp.uint32)


def _const_spec(shape):
    return pl.BlockSpec(shape, lambda i: (0,) * len(shape), pipeline_mode=pl.Buffered(1))


def _mixer(x, g_mix, w_in, conv_w, pool_bd, pool_scale, k, v, w_out, g_ffn, w_up, w_down, *, d_conv, d_pool):
    seq, d_model = x.shape
    rows = MIX_ROWS
    steps = seq // rows
    assert seq % rows == 0 and rows % POOL_HALO == 0
    head_dim = k.shape[2]
    row_spec = pl.BlockSpec((rows, d_model), lambda i: (i, 0))
    consts = (g_mix, conv_w, pool_bd, pool_scale, k, v, g_ffn)
    assert w_in.shape[0] % MIX_STAGE_ROWS == 0 and w_out.shape[0] % MIX_STAGE_ROWS == 0
    assert w_out.shape[1] <= w_in.shape[1]
    d_ff = w_down.shape[0]
    up_chunks = w_up.shape[1] // FFN_COLS
    up_rows = d_model // steps
    down_rows = 2 * d_ff // steps
    assert d_model % steps == 0 and (2 * d_ff) % steps == 0 and steps % 2 == 0
    assert up_rows % BF16_SUBLANES == 0 and down_rows % BF16_SUBLANES == 0 and w_up.shape[1] % FFN_COLS == 0
    return pl.pallas_call(
        functools.partial(_mixer_kernel, d_conv=d_conv, d_pool=d_pool, head_dim=head_dim),
        grid=(steps,),
        in_specs=[row_spec] + [_const_spec(c.shape) for c in consts] + [
            pl.BlockSpec(memory_space=pl.ANY), pl.BlockSpec(memory_space=pl.ANY),
            pl.BlockSpec((up_rows, w_up.shape[1]), lambda i: (i, 0)),
            pl.BlockSpec((down_rows, d_model), lambda i: (i // 2, 0)),
        ],
        out_specs=(row_spec, pl.BlockSpec((rows // 2, d_model), lambda i: (i, 0)),
                   pl.BlockSpec((up_chunks, up_rows, FFN_COLS), lambda i: (0, i, 0)),
                   pl.BlockSpec((down_rows, d_model), lambda i: (i // 2, 0))),
        out_shape=(jax.ShapeDtypeStruct((seq, d_model), jnp.float32),
                   jax.ShapeDtypeStruct((seq // 2, d_model), jnp.uint32),
                   jax.ShapeDtypeStruct((up_chunks, d_model, FFN_COLS), jnp.bfloat16),
                   jax.ShapeDtypeStruct((d_ff, d_model), jnp.bfloat16)),
        scratch_shapes=[
            pltpu.VMEM(w_in.shape, jnp.bfloat16),
            pltpu.VMEM(w_out.shape, jnp.bfloat16),
            pltpu.VMEM((2, MIX_STAGE_ROWS, w_in.shape[1]), jnp.float32),
            pltpu.SemaphoreType.DMA((2,)),
            pltpu.VMEM((rows, w_in.shape[1]), jnp.float32),
            pltpu.VMEM((rows + CONV_HALO, d_conv), jnp.float32),
            pltpu.VMEM((rows + POOL_HALO, d_pool), jnp.float32),
            pltpu.VMEM((rows, d_model), jnp.bfloat16),
        ],
        compiler_params=pltpu.CompilerParams(
            dimension_semantics=("arbitrary",),
            vmem_limit_bytes=_MIXER_VMEM_BYTES,
        ),
        name="token_mixer",
    )(x, *consts, w_in, w_out, w_up, w_down)


def _ffn_kernel(h2_ref, x1_hbm, wg_ref, wv_ref, cwg_ref, cwv_ref, bg_ref, bv_ref, wd_ref, gfin_ref,
                o_ref, gbuf, vbuf, gcarry, vcarry, x1buf, x1_sem):
    i = pl.program_id(0)
    j = pl.program_id(1)
    last_j = pl.num_programs(1) - 1
    rows = o_ref.shape[0]

    def x1_copy():
        return pltpu.make_async_copy(x1_hbm.at[pl.ds(i * rows, rows), :], x1buf, x1_sem)

    @pl.when(i == 0)
    def _():
        gcarry[j] = jnp.zeros(gcarry.shape[1:], jnp.float32)
        vcarry[j] = jnp.zeros(vcarry.shape[1:], jnp.float32)

    @pl.when(j == 0)
    def _():
        x1_copy().start()
        o_ref[...] = jnp.zeros(o_ref.shape, jnp.float32)

    @pl.when(j == last_j)
    def _():
        x1_copy().wait()

    slabs = gbuf.shape[0]
    gbuf[:, 0:CONV_HALO, :] = gcarry[j]
    vbuf[:, 0:CONV_HALO, :] = vcarry[j]
    for w_ref, buf in ((wg_ref, gbuf), (wv_ref, vbuf)):
        for r in range(0, rows, FFN_SUB_ROWS):
            u = _dot(pltpu.bitcast(h2_ref[r // 2:(r + FFN_SUB_ROWS) // 2, :], jnp.bfloat16), w_ref[...])
            for c in range(slabs):
                buf[c, CONV_HALO + r:CONV_HALO + r + FFN_SUB_ROWS, :] = u[:, c * V7X_LANES:(c + 1) * V7X_LANES]
    for r in range(0, rows, FFN_SUB_ROWS):
        part = slice(r, r + FFN_SUB_ROWS)
        halo_part = slice(r, r + CONV_HALO + FFN_SUB_ROWS)
        act = []
        for c in range(slabs):
            lanes = slice(c * V7X_LANES, (c + 1) * V7X_LANES)
            gate = _causal_conv3(gbuf.at[c, halo_part, :], cwg_ref.at[:, lanes], FFN_SUB_ROWS) + bg_ref[:, lanes]
            val = _causal_conv3(vbuf.at[c, halo_part, :], cwv_ref.at[:, lanes], FFN_SUB_ROWS) + bv_ref[:, lanes]
            act.append((gate * jax.nn.sigmoid(gate) * val).astype(jnp.bfloat16))
        o_ref[part, :] += _dot(jnp.concatenate(act, axis=1), wd_ref[...])
    gcarry[j] = gbuf[:, rows:rows + CONV_HALO, :]
    vcarry[j] = vbuf[:, rows:rows + CONV_HALO, :]

    @pl.when(j == last_j)
    def _():
        def norm_rows(c, carry):
            r = pl.multiple_of(c * FFN_NORM_ROWS, FFN_NORM_ROWS)
            y = x1buf[pl.ds(r, FFN_NORM_ROWS), :] + o_ref[pl.ds(r, FFN_NORM_ROWS), :]
            o_ref[pl.ds(r, FFN_NORM_ROWS), :] = _rmsnorm(y, gfin_ref[...])
            return carry

        lax.fori_loop(0, rows // FFN_NORM_ROWS, norm_rows, None)


def _ffn(h2, x1, w_up, ffn_conv_w, ffn_conv_b, w_down, g_final):
    seq, d_model = x1.shape
    d_ff = w_down.shape[0]
    rows, cols = FFN_ROWS, FFN_COLS
    assert seq % rows == 0 and d_ff % cols == 0 and w_up.shape == (2 * d_ff // cols, d_model, cols)
    assert rows % FFN_SUB_ROWS == 0 and rows % FFN_NORM_ROWS == 0
    n_j = d_ff // cols
    row_spec = lambda: pl.BlockSpec((rows, d_model), lambda i, j: (i, 0))
    gate_cols = lambda r: pl.BlockSpec((r, cols), lambda i, j: (0, j))
    val_cols = lambda r: pl.BlockSpec((r, cols), lambda i, j: (0, j + n_j))
    up_gate = pl.BlockSpec((None, d_model, cols), lambda i, j: (j, 0, 0))
    up_val = pl.BlockSpec((None, d_model, cols), lambda i, j: (j + n_j, 0, 0))
    return pl.pallas_call(
        _ffn_kernel,
        grid=(seq // rows, n_j),
        in_specs=[
            pl.BlockSpec((rows // 2, d_model), lambda i, j: (i, 0)), pl.BlockSpec(memory_space=pl.ANY),
            up_gate, up_val,
            gate_cols(CONV_WIDTH), val_cols(CONV_WIDTH),
            gate_cols(1), val_cols(1),
            pl.BlockSpec((cols, d_model), lambda i, j: (j, 0)),
            pl.BlockSpec((1, d_model), lambda i, j: (0, 0)),
        ],
        out_specs=row_spec(),
        out_shape=jax.ShapeDtypeStruct((seq, d_model), jnp.float32),
        scratch_shapes=[
            pltpu.VMEM((cols // V7X_LANES, rows + CONV_HALO, V7X_LANES), jnp.float32),
            pltpu.VMEM((cols // V7X_LANES, rows + CONV_HALO, V7X_LANES), jnp.float32),
            pltpu.VMEM((n_j, cols // V7X_LANES, CONV_HALO, V7X_LANES), jnp.float32),
            pltpu.VMEM((n_j, cols // V7X_LANES, CONV_HALO, V7X_LANES), jnp.float32),
            pltpu.VMEM((rows, d_model), jnp.float32),
            pltpu.SemaphoreType.DMA(()),
        ],
        compiler_params=pltpu.CompilerParams(
            dimension_semantics=("arbitrary", "arbitrary"),
            vmem_limit_bytes=_FFN_VMEM_BYTES,
        ),
        name="ffn",
    )(h2, x1, w_up, w_up, ffn_conv_w, ffn_conv_w, ffn_conv_b, ffn_conv_b, w_down, g_final)


_MIXER_VMEM_BYTES = 56 * 1024 * 1024
_FFN_VMEM_BYTES = 60 * 1024 * 1024


def _block_diag(pool_w):
    groups, dim, _ = pool_w.shape
    bands = [jnp.pad(pool_w[g], ((0, 0), (g * dim, (groups - 1 - g) * dim))) for g in range(groups)]
    return jnp.concatenate(bands, axis=0)


def kernel(x, mem, g_mix, g_mem, w_in, conv_w, pool_w, pool_scale, w_kv, w_out, g_ffn, w_up,
           ffn_conv_w, ffn_conv_b, w_down, g_final):
    assert x.shape[0] == 1 and mem.shape[0] == 1 and w_in.shape[0] == 1
    assert tuple(sorted(POOL_WINDOWS)) == POOL_WINDOWS and pool_w.shape[1] == len(POOL_WINDOWS)
    bf16 = jnp.bfloat16
    d_conv = conv_w.shape[2]
    d_pool = pool_scale.shape[1]
    k, v = _kv_proj(mem[0], g_mem[0][None, :], w_kv[0])
    x1, h2, w_up16, w_down16 = _mixer(
        x[0], g_mix[0][None, :], w_in[0], conv_w[0], _block_diag(pool_w[0]).astype(bf16),
        pool_scale[0][None, :], k, v, w_out[0], g_ffn[0][None, :], w_up[0], w_down[0],
        d_conv=d_conv, d_pool=d_pool)
    out = _ffn(h2, x1, w_up16, ffn_conv_w[0], ffn_conv_b[0][None, :], w_down16, g_final[None, :])
    return out[None]
```

```python
import functools
import math

import jax
import jax.numpy as jnp
from jax import lax
from jax.experimental import pallas as pl
from jax.experimental.pallas import tpu as pltpu

EPS = 1e-6
POOL_WINDOWS = (2, 4, 8, 16)
XATT_HEADS = 4
CONV_WIDTH = 3

V7X_SUBLANES = 8
V7X_LANES = 128
BF16_SUBLANES = 2 * V7X_SUBLANES
V7X_VMEM_BYTES = 64 * 1024 * 1024

MIX_ROWS = 256
MIX_STAGE_ROWS = 128
MIX_STAGE_SLOTS = 4
FFN_ROWS = 1024
FFN_SUB_ROWS = 512
FFN_NORM_ROWS = 128
FFN_COLS = 512
POOL_HALO = 16
CONV_HALO = V7X_SUBLANES


def _rmsnorm(x, g):
    return x * lax.rsqrt(jnp.mean(x * x, axis=-1, keepdims=True) + EPS) * g


def _dot(a, b):
    return jnp.dot(a, b, preferred_element_type=jnp.float32)


def _causal_conv3(buf_ref, w_ref, rows):
    out = w_ref[2:3, :] * buf_ref[CONV_HALO:CONV_HALO + rows, :]
    for j in range(CONV_WIDTH - 1):
        lag = CONV_WIDTH - 1 - j
        out = out + w_ref[j:j + 1, :] * buf_ref[CONV_HALO - lag:CONV_HALO - lag + rows, :]
    return out


def _kv_kernel(mem_ref, g_ref, w_ref, k_ref, v_ref):
    m = _rmsnorm(mem_ref[...], g_ref[...]).astype(jnp.bfloat16)
    kv = _dot(m, w_ref[...].astype(jnp.bfloat16))
    heads, _, head_dim = k_ref.shape
    for hd in range(heads):
        k_ref[hd] = kv[:, hd * head_dim:(hd + 1) * head_dim].astype(jnp.bfloat16)
        v_ref[hd] = kv[:, (heads + hd) * head_dim:(heads + hd + 1) * head_dim].astype(jnp.bfloat16)


def _kv_proj(mem, g_mem, w_kv):
    n_mem, _ = mem.shape
    head_dim = w_kv.shape[1] // (2 * XATT_HEADS)
    out = jax.ShapeDtypeStruct((XATT_HEADS, n_mem, head_dim), jnp.bfloat16)
    return pl.pallas_call(_kv_kernel, out_shape=(out, out), name="kv_proj")(mem, g_mem, w_kv)


def _load_as_bf16(w_hbm, w16, stage, sems):
    slots, chunk, _ = stage.shape
    n_chunks = w_hbm.shape[0] // chunk
    cols = w_hbm.shape[1]
    ahead = slots - 1
    assert n_chunks >= ahead

    def copy(c):
        slot = c % slots
        return pltpu.make_async_copy(w_hbm.at[pl.ds(pl.multiple_of(c * chunk, chunk), chunk), :],
                                     stage.at[slot, :, pl.ds(0, cols)], sems.at[slot])

    for c in range(ahead):
        copy(c).start()

    def step(c, carry):
        @pl.when(c + ahead < n_chunks)
        def _():
            copy(c + ahead).start()

        copy(c).wait()
        w16[pl.ds(pl.multiple_of(c * chunk, chunk), chunk), :] = stage[c % slots, :, 0:cols].astype(jnp.bfloat16)
        return carry

    lax.fori_loop(0, n_chunks, step, None)


def _mixer_kernel(x_ref, gmix_ref, convw_ref, poolw_ref, pscale_ref, k_ref, v_ref, gffn_ref, win_hbm, wout_hbm,
                  wup_ref, wdown_ref, x1_ref, h2_ref, wup16_ref, wdown16_ref,
                  win_ref, wout_ref, stage, stage_sems, zbuf, pbuf, vbuf, mixbuf, *, d_conv, d_pool, head_dim):
    i = pl.program_id(0)
    rows = x_ref.shape[0]

    @pl.when(i == 0)
    def _():
        pbuf[0:CONV_HALO, :] = jnp.zeros((CONV_HALO, d_conv), jnp.float32)
        vbuf[0:POOL_HALO, :] = jnp.zeros((POOL_HALO, d_pool), jnp.float32)
        _load_as_bf16(win_hbm, win_ref, stage, stage_sems)
        _load_as_bf16(wout_hbm, wout_ref, stage, stage_sems)

    x = x_ref[...]
    h = _rmsnorm(x, gmix_ref[...]).astype(jnp.bfloat16)
    zbuf[...] = _dot(h, win_ref[...])

    def cast_up_chunks(lo, hi):
        ff_cols = wup16_ref.shape[2]
        for c in range(lo, hi):
            wup16_ref[c] = wup_ref[:, c * ff_cols:(c + 1) * ff_cols].astype(jnp.bfloat16)

    up_chunks = wup16_ref.shape[0]
    cast_up_chunks(0, up_chunks // 3)

    c0 = d_conv
    pbuf[CONV_HALO:CONV_HALO + rows, :] = zbuf[:, c0:2 * c0] * zbuf[:, 2 * c0:3 * c0]
    conv = _causal_conv3(pbuf, convw_ref, rows)
    pbuf[0:CONV_HALO, :] = pbuf[rows:rows + CONV_HALO, :]
    mixbuf[:, 0:c0] = (zbuf[:, 0:c0] * conv).astype(jnp.bfloat16)

    cast_up_chunks(up_chunks // 3, 2 * up_chunks // 3)

    p0 = 3 * d_conv
    vbuf[POOL_HALO:POOL_HALO + rows, :] = zbuf[:, p0:p0 + d_pool]
    group_dim = d_pool // len(POOL_WINDOWS)
    t = (i * rows + lax.broadcasted_iota(jnp.int32, (rows, 1), 0)).astype(jnp.float32)
    for c in range(0, d_pool, V7X_LANES):
        col = c + lax.broadcasted_iota(jnp.int32, (1, V7X_LANES), 1)
        cur = vbuf[POOL_HALO:POOL_HALO + rows, c:c + V7X_LANES]
        acc = cur
        lag = 1
        pooled = None
        for g in range(c // group_dim, (c + V7X_LANES - 1) // group_dim + 1):
            k = POOL_WINDOWS[g]
            while lag < k:
                acc = acc + vbuf[POOL_HALO - lag:POOL_HALO - lag + rows, c:c + V7X_LANES]
                lag += 1
            mean_k = acc / jnp.minimum(t + 1.0, float(k))
            pooled = mean_k if pooled is None else jnp.where(col >= g * group_dim, mean_k, pooled)
        mixbuf[:, c0 + c:c0 + c + V7X_LANES] = (pooled - cur).astype(jnp.bfloat16)
    vbuf[0:POOL_HALO, :] = vbuf[rows:rows + POOL_HALO, :]
    pool_out = _dot(mixbuf[:, c0:c0 + d_pool], poolw_ref[...]) * pscale_ref[...]
    mixbuf[:, c0:c0 + d_pool] = pool_out.astype(jnp.bfloat16)

    cast_up_chunks(2 * up_chunks // 3, up_chunks)

    q0 = p0 + d_pool
    a0 = c0 + d_pool
    q = jnp.stack([zbuf[:, q0 + hd * head_dim:q0 + (hd + 1) * head_dim] for hd in range(XATT_HEADS)])
    s = jnp.einsum("hqd,hmd->hqm", q.astype(jnp.bfloat16), k_ref[...],
                   preferred_element_type=jnp.float32) * (1.0 / math.sqrt(head_dim))
    e = jnp.exp(s - jnp.max(s, axis=-1, keepdims=True))
    probs = (e / jnp.sum(e, axis=-1, keepdims=True)).astype(jnp.bfloat16)
    att = jnp.einsum("hqm,hmd->hqd", probs, v_ref[...], preferred_element_type=jnp.float32)
    for hd in range(XATT_HEADS):
        mixbuf[:, a0 + hd * head_dim:a0 + (hd + 1) * head_dim] = att[hd].astype(jnp.bfloat16)

    wdown16_ref[...] = wdown_ref[...].astype(jnp.bfloat16)
    x1 = x + _dot(mixbuf[...], wout_ref[...])
    x1_ref[...] = x1
    h2_ref[...] = pltpu.bitcast(_rmsnorm(x1, gffn_ref[...]).astype(jnp.bfloat16), jnp.uint32)


def _const_spec(shape):
    return pl.BlockSpec(shape, lambda i: (0,) * len(shape), pipeline_mode=pl.Buffered(1))


def _mixer(x, g_mix, w_in, conv_w, pool_bd, pool_scale, k, v, w_out, g_ffn, w_up, w_down, *, d_conv, d_pool):
    seq, d_model = x.shape
    rows = MIX_ROWS
    steps = seq // rows
    assert seq % rows == 0 and rows % POOL_HALO == 0
    head_dim = k.shape[2]
    row_spec = pl.BlockSpec((rows, d_model), lambda i: (i, 0))
    consts = (g_mix, conv_w, pool_bd, pool_scale, k, v, g_ffn)
    assert w_in.shape[0] % MIX_STAGE_ROWS == 0 and w_out.shape[0] % MIX_STAGE_ROWS == 0
    assert w_out.shape[1] <= w_in.shape[1]
    d_ff = w_down.shape[0]
    up_chunks = w_up.shape[1] // FFN_COLS
    up_rows = d_model // steps
    down_rows = 2 * d_ff // steps
    assert d_model % steps == 0 and (2 * d_ff) % steps == 0 and steps % 2 == 0
    assert up_rows % BF16_SUBLANES == 0 and down_rows % BF16_SUBLANES == 0 and w_up.shape[1] % FFN_COLS == 0
    return pl.pallas_call(
        functools.partial(_mixer_kernel, d_conv=d_conv, d_pool=d_pool, head_dim=head_dim),
        grid=(steps,),
        in_specs=[row_spec] + [_const_spec(c.shape) for c in consts] + [
            pl.BlockSpec(memory_space=pl.ANY), pl.BlockSpec(memory_space=pl.ANY),
            pl.BlockSpec((up_rows, w_up.shape[1]), lambda i: (i, 0)),
            pl.BlockSpec((down_rows, d_model), lambda i: (i // 2, 0)),
        ],
        out_specs=(row_spec, pl.BlockSpec((rows // 2, d_model), lambda i: (i, 0)),
                   pl.BlockSpec((up_chunks, up_rows, FFN_COLS), lambda i: (0, i, 0)),
                   pl.BlockSpec((down_rows, d_model), lambda i: (i // 2, 0))),
        out_shape=(jax.ShapeDtypeStruct((seq, d_model), jnp.float32),
                   jax.ShapeDtypeStruct((seq // 2, d_model), jnp.uint32),
                   jax.ShapeDtypeStruct((up_chunks, d_model, FFN_COLS), jnp.bfloat16),
                   jax.ShapeDtypeStruct((d_ff, d_model), jnp.bfloat16)),
        scratch_shapes=[
            pltpu.VMEM(w_in.shape, jnp.bfloat16),
            pltpu.VMEM(w_out.shape, jnp.bfloat16),
            pltpu.VMEM((MIX_STAGE_SLOTS, MIX_STAGE_ROWS, w_in.shape[1]), jnp.float32),
            pltpu.SemaphoreType.DMA((MIX_STAGE_SLOTS,)),
            pltpu.VMEM((rows, w_in.shape[1]), jnp.float32),
            pltpu.VMEM((rows + CONV_HALO, d_conv), jnp.float32),
            pltpu.VMEM((rows + POOL_HALO, d_pool), jnp.float32),
            pltpu.VMEM((rows, d_model), jnp.bfloat16),
        ],
        compiler_params=pltpu.CompilerParams(
            dimension_semantics=("arbitrary",),
            vmem_limit_bytes=_MIXER_VMEM_BYTES,
        ),
        name="token_mixer",
    )(x, *consts, w_in, w_out, w_up, w_down)


def _ffn_kernel(h2_ref, x1_hbm, wg_ref, wv_ref, cwg_ref, cwv_ref, bg_ref, bv_ref, wd_ref, gfin_ref,
                o_ref, gbuf, vbuf, gcarry, vcarry, x1buf, x1_sem):
    i = pl.program_id(0)
    j = pl.program_id(1)
    last_j = pl.num_programs(1) - 1
    rows = o_ref.shape[0]

    def x1_copy():
        return pltpu.make_async_copy(x1_hbm.at[pl.ds(i * rows, rows), :], x1buf, x1_sem)

    @pl.when(i == 0)
    def _():
        gcarry[j] = jnp.zeros(gcarry.shape[1:], jnp.float32)
        vcarry[j] = jnp.zeros(vcarry.shape[1:], jnp.float32)

    @pl.when(j == 0)
    def _():
        x1_copy().start()
        o_ref[...] = jnp.zeros(o_ref.shape, jnp.float32)

    @pl.when(j == last_j)
    def _():
        x1_copy().wait()

    slabs = gbuf.shape[0]
    gbuf[:, 0:CONV_HALO, :] = gcarry[j]
    vbuf[:, 0:CONV_HALO, :] = vcarry[j]
    for w_ref, buf in ((wg_ref, gbuf), (wv_ref, vbuf)):
        for r in range(0, rows, FFN_SUB_ROWS):
            u = _dot(pltpu.bitcast(h2_ref[r // 2:(r + FFN_SUB_ROWS) // 2, :], jnp.bfloat16), w_ref[...])
            for c in range(slabs):
                buf[c, CONV_HALO + r:CONV_HALO + r + FFN_SUB_ROWS, :] = u[:, c * V7X_LANES:(c + 1) * V7X_LANES]
    for r in range(0, rows, FFN_SUB_ROWS):
        part = slice(r, r + FFN_SUB_ROWS)
        halo_part = slice(r, r + CONV_HALO + FFN_SUB_ROWS)
        act = []
        for c in range(slabs):
            lanes = slice(c * V7X_LANES, (c + 1) * V7X_LANES)
            gate = _causal_conv3(gbuf.at[c, halo_part, :], cwg_ref.at[:, lanes], FFN_SUB_ROWS) + bg_ref[:, lanes]
            val = _causal_conv3(vbuf.at[c, halo_part, :], cwv_ref.at[:, lanes], FFN_SUB_ROWS) + bv_ref[:, lanes]
            act.append((gate * jax.nn.sigmoid(gate) * val).astype(jnp.bfloat16))
        o_ref[part, :] += _dot(jnp.concatenate(act, axis=1), wd_ref[...])
    gcarry[j] = gbuf[:, rows:rows + CONV_HALO, :]
    vcarry[j] = vbuf[:, rows:rows + CONV_HALO, :]

    @pl.when(j == last_j)
    def _():
        def norm_rows(c, carry):
            r = pl.multiple_of(c * FFN_NORM_ROWS, FFN_NORM_ROWS)
            y = x1buf[pl.ds(r, FFN_NORM_ROWS), :] + o_ref[pl.ds(r, FFN_NORM_ROWS), :]
            o_ref[pl.ds(r, FFN_NORM_ROWS), :] = _rmsnorm(y, gfin_ref[...])
            return carry

        lax.fori_loop(0, rows // FFN_NORM_ROWS, norm_rows, None)


def _ffn(h2, x1, w_up, ffn_conv_w, ffn_conv_b, w_down, g_final):
    seq, d_model = x1.shape
    d_ff = w_down.shape[0]
    rows, cols = FFN_ROWS, FFN_COLS
    assert seq % rows == 0 and d_ff % cols == 0 and w_up.shape == (2 * d_ff // cols, d_model, cols)
    assert rows % FFN_SUB_ROWS == 0 and rows % FFN_NORM_ROWS == 0
    n_j = d_ff // cols
    row_spec = lambda: pl.BlockSpec((rows, d_model), lambda i, j: (i, 0))
    gate_cols = lambda r: pl.BlockSpec((r, cols), lambda i, j: (0, j))
    val_cols = lambda r: pl.BlockSpec((r, cols), lambda i, j: (0, j + n_j))
    up_gate = pl.BlockSpec((None, d_model, cols), lambda i, j: (j, 0, 0))
    up_val = pl.BlockSpec((None, d_model, cols), lambda i, j: (j + n_j, 0, 0))
    return pl.pallas_call(
        _ffn_kernel,
        grid=(seq // rows, n_j),
        in_specs=[
            pl.BlockSpec((rows // 2, d_model), lambda i, j: (i, 0)), pl.BlockSpec(memory_space=pl.ANY),
            up_gate, up_val,
            gate_cols(CONV_WIDTH), val_cols(CONV_WIDTH),
            gate_cols(1), val_cols(1),
            pl.BlockSpec((cols, d_model), lambda i, j: (j, 0)),
            pl.BlockSpec((1, d_model), lambda i, j: (0, 0)),
        ],
        out_specs=row_spec(),
        out_shape=jax.ShapeDtypeStruct((seq, d_model), jnp.float32),
        scratch_shapes=[
            pltpu.VMEM((cols // V7X_LANES, rows + CONV_HALO, V7X_LANES), jnp.float32),
            pltpu.VMEM((cols // V7X_LANES, rows + CONV_HALO, V7X_LANES), jnp.float32),
            pltpu.VMEM((n_j, cols // V7X_LANES, CONV_HALO, V7X_LANES), jnp.float32),
            pltpu.VMEM((n_j, cols // V7X_LANES, CONV_HALO, V7X_LANES), jnp.float32),
            pltpu.VMEM((rows, d_model), jnp.float32),
            pltpu.SemaphoreType.DMA(()),
        ],
        compiler_params=pltpu.CompilerParams(
            dimension_semantics=("arbitrary", "arbitrary"),
            vmem_limit_bytes=_FFN_VMEM_BYTES,
        ),
        name="ffn",
    )(h2, x1, w_up, w_up, ffn_conv_w, ffn_conv_w, ffn_conv_b, ffn_conv_b, w_down, g_final)


_MIXER_VMEM_BYTES = 56 * 1024 * 1024
_FFN_VMEM_BYTES = 60 * 1024 * 1024


def _block_diag(pool_w):
    groups, dim, _ = pool_w.shape
    bands = [jnp.pad(pool_w[g], ((0, 0), (g * dim, (groups - 1 - g) * dim))) for g in range(groups)]
    return jnp.concatenate(bands, axis=0)


def kernel(x, mem, g_mix, g_mem, w_in, conv_w, pool_w, pool_scale, w_kv, w_out, g_ffn, w_up,
           ffn_conv_w, ffn_conv_b, w_down, g_final):
    assert x.shape[0] == 1 and mem.shape[0] == 1 and w_in.shape[0] == 1
    assert tuple(sorted(POOL_WINDOWS)) == POOL_WINDOWS and pool_w.shape[1] == len(POOL_WINDOWS)
    bf16 = jnp.bfloat16
    d_conv = conv_w.shape[2]
    d_pool = pool_scale.shape[1]
    k, v = _kv_proj(mem[0], g_mem[0][None, :], w_kv[0])
    x1, h2, w_up16, w_down16 = _mixer(
        x[0], g_mix[0][None, :], w_in[0], conv_w[0], _block_diag(pool_w[0]).astype(bf16),
        pool_scale[0][None, :], k, v, w_out[0], g_ffn[0][None, :], w_up[0], w_down[0],
        d_conv=d_conv, d_pool=d_pool)
    out = _ffn(h2, x1, w_up16, ffn_conv_w[0], ffn_conv_b[0][None, :], w_down16, g_final[None, :])
    return out[None]
```

```python
import functools
import math

import jax
import jax.numpy as jnp
from jax import lax
from jax.experimental import pallas as pl
from jax.experimental.pallas import tpu as pltpu

EPS = 1e-6
POOL_WINDOWS = (2, 4, 8, 16)
XATT_HEADS = 4
CONV_WIDTH = 3

V7X_SUBLANES = 8
V7X_LANES = 128
BF16_SUBLANES = 2 * V7X_SUBLANES
V7X_VMEM_BYTES = 64 * 1024 * 1024

MIX_ROWS = 256
MIX_STAGE_ROWS = 128
MIX_STAGE_SLOTS = 4
FFN_ROWS = 1024
FFN_SUB_ROWS = 512
FFN_NORM_ROWS = 128
FFN_COLS = 512
POOL_HALO = 16
CONV_HALO = V7X_SUBLANES


def _rmsnorm(x, g):
    return x * lax.rsqrt(jnp.mean(x * x, axis=-1, keepdims=True) + EPS) * g


def _dot(a, b):
    return jnp.dot(a, b, preferred_element_type=jnp.float32)


def _causal_conv3(buf_ref, w_ref, rows):
    out = w_ref[2:3, :] * buf_ref[CONV_HALO:CONV_HALO + rows, :]
    for j in range(CONV_WIDTH - 1):
        lag = CONV_WIDTH - 1 - j
        out = out + w_ref[j:j + 1, :] * buf_ref[CONV_HALO - lag:CONV_HALO - lag + rows, :]
    return out


def _kv_kernel(mem_ref, g_ref, w_ref, k_ref, v_ref):
    m = _rmsnorm(mem_ref[...], g_ref[...]).astype(jnp.bfloat16)
    kv = _dot(m, w_ref[...].astype(jnp.bfloat16))
    heads, _, head_dim = k_ref.shape
    for hd in range(heads):
        k_ref[hd] = kv[:, hd * head_dim:(hd + 1) * head_dim].astype(jnp.bfloat16)
        v_ref[hd] = kv[:, (heads + hd) * head_dim:(heads + hd + 1) * head_dim].astype(jnp.bfloat16)


def _kv_proj(mem, g_mem, w_kv):
    n_mem, _ = mem.shape
    head_dim = w_kv.shape[1] // (2 * XATT_HEADS)
    out = jax.ShapeDtypeStruct((XATT_HEADS, n_mem, head_dim), jnp.bfloat16)
    return pl.pallas_call(_kv_kernel, out_shape=(out, out), name="kv_proj")(mem, g_mem, w_kv)


def _load_as_bf16(w_hbm, w16, stage, sems):
    slots, chunk, _ = stage.shape
    n_chunks = w_hbm.shape[0] // chunk
    cols = w_hbm.shape[1]
    ahead = slots - 1
    assert n_chunks >= ahead

    def copy(c):
        slot = c % slots
        return pltpu.make_async_copy(w_hbm.at[pl.ds(pl.multiple_of(c * chunk, chunk), chunk), :],
                                     stage.at[slot, :, pl.ds(0, cols)], sems.at[slot])

    for c in range(ahead):
        copy(c).start()

    def step(c, carry):
        @pl.when(c + ahead < n_chunks)
        def _():
            copy(c + ahead).start()

        copy(c).wait()
        w16[pl.ds(pl.multiple_of(c * chunk, chunk), chunk), :] = stage[c % slots, :, 0:cols].astype(jnp.bfloat16)
        return carry

    lax.fori_loop(0, n_chunks, step, None)


def _mixer_kernel(x_ref, gmix_ref, convw_ref, poolw_ref, pscale_ref, k_ref, v_ref, gffn_ref, win_hbm, wout_hbm,
                  wup_ref, wdown_ref, x1_ref, h2_ref, wup16_ref, wdown16_ref,
                  win_ref, wout_ref, stage, stage_sems, zbuf, pbuf, vbuf, mixbuf, *, d_conv, d_pool, head_dim):
    i = pl.program_id(0)
    rows = x_ref.shape[0]

    @pl.when(i == 0)
    def _():
        pbuf[0:CONV_HALO, :] = jnp.zeros((CONV_HALO, d_conv), jnp.float32)
        vbuf[0:POOL_HALO, :] = jnp.zeros((POOL_HALO, d_pool), jnp.float32)
        _load_as_bf16(win_hbm, win_ref, stage, stage_sems)
        _load_as_bf16(wout_hbm, wout_ref, stage, stage_sems)

    x = x_ref[...]
    h = _rmsnorm(x, gmix_ref[...]).astype(jnp.bfloat16)
    zbuf[...] = _dot(h, win_ref[...])

    def cast_up_chunks(lo, hi):
        ff_cols = wup16_ref.shape[2]
        for c in range(lo, hi):
            wup16_ref[c] = wup_ref[:, c * ff_cols:(c + 1) * ff_cols].astype(jnp.bfloat16)

    up_chunks = wup16_ref.shape[0]
    cast_up_chunks(0, up_chunks // 3)

    c0 = d_conv
    pbuf[CONV_HALO:CONV_HALO + rows, :] = zbuf[:, c0:2 * c0] * zbuf[:, 2 * c0:3 * c0]
    conv = _causal_conv3(pbuf, convw_ref, rows)
    pbuf[0:CONV_HALO, :] = pbuf[rows:rows + CONV_HALO, :]
    mixbuf[:, 0:c0] = (zbuf[:, 0:c0] * conv).astype(jnp.bfloat16)

    cast_up_chunks(up_chunks // 3, 2 * up_chunks // 3)

    p0 = 3 * d_conv
    vbuf[POOL_HALO:POOL_HALO + rows, :] = zbuf[:, p0:p0 + d_pool]
    group_dim = d_pool // len(POOL_WINDOWS)
    t = (i * rows + lax.broadcasted_iota(jnp.int32, (rows, 1), 0)).astype(jnp.float32)
    for c in range(0, d_pool, V7X_LANES):
        col = c + lax.broadcasted_iota(jnp.int32, (1, V7X_LANES), 1)
        cur = vbuf[POOL_HALO:POOL_HALO + rows, c:c + V7X_LANES]
        acc = cur
        lag = 1
        pooled = None
        for g in range(c // group_dim, (c + V7X_LANES - 1) // group_dim + 1):
            k = POOL_WINDOWS[g]
            while lag < k:
                acc = acc + vbuf[POOL_HALO - lag:POOL_HALO - lag + rows, c:c + V7X_LANES]
                lag += 1
            mean_k = acc / jnp.minimum(t + 1.0, float(k))
            pooled = mean_k if pooled is None else jnp.where(col >= g * group_dim, mean_k, pooled)
        mixbuf[:, c0 + c:c0 + c + V7X_LANES] = (pooled - cur).astype(jnp.bfloat16)
    vbuf[0:POOL_HALO, :] = vbuf[rows:rows + POOL_HALO, :]
    pool_out = _dot(mixbuf[:, c0:c0 + d_pool], poolw_ref[...]) * pscale_ref[...]
    mixbuf[:, c0:c0 + d_pool] = pool_out.astype(jnp.bfloat16)

    cast_up_chunks(2 * up_chunks // 3, up_chunks)

    q0 = p0 + d_pool
    a0 = c0 + d_pool
    q = jnp.stack([zbuf[:, q0 + hd * head_dim:q0 + (hd + 1) * head_dim] for hd in range(XATT_HEADS)])
    s = jnp.einsum("hqd,hmd->hqm", q.astype(jnp.bfloat16), k_ref[...],
                   preferred_element_type=jnp.float32) * (1.0 / math.sqrt(head_dim))
    e = jnp.exp(s - jnp.max(s, axis=-1, keepdims=True))
    probs = (e / jnp.sum(e, axis=-1, keepdims=True)).astype(jnp.bfloat16)
    att = jnp.einsum("hqm,hmd->hqd", probs, v_ref[...], preferred_element_type=jnp.float32)
    for hd in range(XATT_HEADS):
        mixbuf[:, a0 + hd * head_dim:a0 + (hd + 1) * head_dim] = att[hd].astype(jnp.bfloat16)

    wdown16_ref[...] = wdown_ref[...].astype(jnp.bfloat16)
    x1 = x + _dot(mixbuf[...], wout_ref[...])
    x1_ref[...] = x1
    h2_ref[...] = pltpu.bitcast(_rmsnorm(x1, gffn_ref[...]).astype(jnp.bfloat16), jnp.uint32)


def _const_spec(shape):
    return pl.BlockSpec(shape, lambda i: (0,) * len(shape), pipeline_mode=pl.Buffered(1))


def _mixer(x, g_mix, w_in, conv_w, pool_bd, pool_scale, k, v, w_out, g_ffn, w_up, w_down, *, d_conv, d_pool):
    seq, d_model = x.shape
    rows = MIX_ROWS
    steps = seq // rows
    assert seq % rows == 0 and rows % POOL_HALO == 0
    head_dim = k.shape[2]
    row_spec = pl.BlockSpec((rows, d_model), lambda i: (i, 0))
    consts = (g_mix, conv_w, pool_bd, pool_scale, k, v, g_ffn)
    const_specs = [_const_spec(c.shape) for c in consts]
    const_specs[1] = pl.BlockSpec((None,) + conv_w.shape[1:], lambda i: (0, 0, 0), pipeline_mode=pl.Buffered(1))
    assert w_in.shape[0] % MIX_STAGE_ROWS == 0 and w_out.shape[0] % MIX_STAGE_ROWS == 0
    assert w_out.shape[1] <= w_in.shape[1]
    d_ff = w_down.shape[0]
    up_chunks = w_up.shape[1] // FFN_COLS
    up_rows = d_model // steps
    down_rows = 2 * d_ff // steps
    assert d_model % steps == 0 and (2 * d_ff) % steps == 0 and steps % 2 == 0
    assert up_rows % BF16_SUBLANES == 0 and down_rows % BF16_SUBLANES == 0 and w_up.shape[1] % FFN_COLS == 0
    return pl.pallas_call(
        functools.partial(_mixer_kernel, d_conv=d_conv, d_pool=d_pool, head_dim=head_dim),
        grid=(steps,),
        in_specs=[row_spec] + const_specs + [
            pl.BlockSpec(memory_space=pl.ANY), pl.BlockSpec(memory_space=pl.ANY),
            pl.BlockSpec((up_rows, w_up.shape[1]), lambda i: (i, 0)),
            pl.BlockSpec((down_rows, d_model), lambda i: (i // 2, 0)),
        ],
        out_specs=(row_spec, pl.BlockSpec((rows // 2, d_model), lambda i: (i, 0)),
                   pl.BlockSpec((up_chunks, up_rows, FFN_COLS), lambda i: (0, i, 0)),
                   pl.BlockSpec((down_rows, d_model), lambda i: (i // 2, 0))),
        out_shape=(jax.ShapeDtypeStruct((seq, d_model), jnp.float32),
                   jax.ShapeDtypeStruct((seq // 2, d_model), jnp.uint32),
                   jax.ShapeDtypeStruct((up_chunks, d_model, FFN_COLS), jnp.bfloat16),
                   jax.ShapeDtypeStruct((d_ff, d_model), jnp.bfloat16)),
        scratch_shapes=[
            pltpu.VMEM(w_in.shape, jnp.bfloat16),
            pltpu.VMEM(w_out.shape, jnp.bfloat16),
            pltpu.VMEM((MIX_STAGE_SLOTS, MIX_STAGE_ROWS, w_in.shape[1]), jnp.float32),
            pltpu.SemaphoreType.DMA((MIX_STAGE_SLOTS,)),
            pltpu.VMEM((rows, w_in.shape[1]), jnp.float32),
            pltpu.VMEM((rows + CONV_HALO, d_conv), jnp.float32),
            pltpu.VMEM((rows + POOL_HALO, d_pool), jnp.float32),
            pltpu.VMEM((rows, d_model), jnp.bfloat16),
        ],
        compiler_params=pltpu.CompilerParams(
            dimension_semantics=("arbitrary",),
            vmem_limit_bytes=_MIXER_VMEM_BYTES,
        ),
        name="token_mixer",
    )(x, *consts, w_in, w_out, w_up, w_down)


def _ffn_kernel(h2_ref, x1_hbm, wg_ref, wv_ref, cwg_ref, cwv_ref, bg_ref, bv_ref, wd_ref, gfin_ref,
                o_ref, gbuf, vbuf, gcarry, vcarry, x1buf, x1_sem):
    i = pl.program_id(0)
    j = pl.program_id(1)
    last_j = pl.num_programs(1) - 1
    rows = o_ref.shape[0]

    def x1_copy():
        return pltpu.make_async_copy(x1_hbm.at[pl.ds(i * rows, rows), :], x1buf, x1_sem)

    @pl.when(i == 0)
    def _():
        gcarry[j] = jnp.zeros(gcarry.shape[1:], jnp.float32)
        vcarry[j] = jnp.zeros(vcarry.shape[1:], jnp.float32)

    @pl.when(j == 0)
    def _():
        x1_copy().start()
        o_ref[...] = jnp.zeros(o_ref.shape, jnp.float32)

    @pl.when(j == last_j)
    def _():
        x1_copy().wait()

    slabs = gbuf.shape[0]
    gbuf[:, 0:CONV_HALO, :] = gcarry[j]
    vbuf[:, 0:CONV_HALO, :] = vcarry[j]
    for w_ref, buf in ((wg_ref, gbuf), (wv_ref, vbuf)):
        for r in range(0, rows, FFN_SUB_ROWS):
            u = _dot(pltpu.bitcast(h2_ref[r // 2:(r + FFN_SUB_ROWS) // 2, :], jnp.bfloat16), w_ref[...])
            for c in range(slabs):
                buf[c, CONV_HALO + r:CONV_HALO + r + FFN_SUB_ROWS, :] = u[:, c * V7X_LANES:(c + 1) * V7X_LANES]
    for r in range(0, rows, FFN_SUB_ROWS):
        part = slice(r, r + FFN_SUB_ROWS)
        halo_part = slice(r, r + CONV_HALO + FFN_SUB_ROWS)
        act = []
        for c in range(slabs):
            lanes = slice(c * V7X_LANES, (c + 1) * V7X_LANES)
            gate = _causal_conv3(gbuf.at[c, halo_part, :], cwg_ref.at[:, lanes], FFN_SUB_ROWS) + bg_ref[:, lanes]
            val = _causal_conv3(vbuf.at[c, halo_part, :], cwv_ref.at[:, lanes], FFN_SUB_ROWS) + bv_ref[:, lanes]
            act.append((gate * jax.nn.sigmoid(gate) * val).astype(jnp.bfloat16))
        o_ref[part, :] += _dot(jnp.concatenate(act, axis=1), wd_ref[...])
    gcarry[j] = gbuf[:, rows:rows + CONV_HALO, :]
    vcarry[j] = vbuf[:, rows:rows + CONV_HALO, :]

    @pl.when(j == last_j)
    def _():
        def norm_rows(c, carry):
            r = pl.multiple_of(c * FFN_NORM_ROWS, FFN_NORM_ROWS)
            y = x1buf[pl.ds(r, FFN_NORM_ROWS), :] + o_ref[pl.ds(r, FFN_NORM_ROWS), :]
            o_ref[pl.ds(r, FFN_NORM_ROWS), :] = _rmsnorm(y, gfin_ref[...])
            return carry

        lax.fori_loop(0, rows // FFN_NORM_ROWS, norm_rows, None)


def _ffn(h2, x1, w_up, ffn_conv_w, ffn_conv_b, w_down, g_final):
    seq, d_model = x1.shape
    d_ff = w_down.shape[0]
    rows, cols = FFN_ROWS, FFN_COLS
    assert seq % rows == 0 and d_ff % cols == 0 and w_up.shape == (2 * d_ff // cols, d_model, cols)
    assert rows % FFN_SUB_ROWS == 0 and rows % FFN_NORM_ROWS == 0
    n_j = d_ff // cols
    row_spec = lambda: pl.BlockSpec((rows, d_model), lambda i, j: (i, 0))
    gate_cols = lambda r: pl.BlockSpec((r, cols), lambda i, j: (0, j))
    val_cols = lambda r: pl.BlockSpec((r, cols), lambda i, j: (0, j + n_j))
    up_gate = pl.BlockSpec((None, d_model, cols), lambda i, j: (j, 0, 0))
    up_val = pl.BlockSpec((None, d_model, cols), lambda i, j: (j + n_j, 0, 0))
    return pl.pallas_call(
        _ffn_kernel,
        grid=(seq // rows, n_j),
        in_specs=[
            pl.BlockSpec((rows // 2, d_model), lambda i, j: (i, 0)), pl.BlockSpec(memory_space=pl.ANY),
            up_gate, up_val,
            pl.BlockSpec((None, CONV_WIDTH, cols), lambda i, j: (0, 0, j)),
            pl.BlockSpec((None, CONV_WIDTH, cols), lambda i, j: (0, 0, j + n_j)),
            gate_cols(1), val_cols(1),
            pl.BlockSpec((cols, d_model), lambda i, j: (j, 0)),
            pl.BlockSpec((1, d_model), lambda i, j: (0, 0)),
        ],
        out_specs=row_spec(),
        out_shape=jax.ShapeDtypeStruct((seq, d_model), jnp.float32),
        scratch_shapes=[
            pltpu.VMEM((cols // V7X_LANES, rows + CONV_HALO, V7X_LANES), jnp.float32),
            pltpu.VMEM((cols // V7X_LANES, rows + CONV_HALO, V7X_LANES), jnp.float32),
            pltpu.VMEM((n_j, cols // V7X_LANES, CONV_HALO, V7X_LANES), jnp.float32),
            pltpu.VMEM((n_j, cols // V7X_LANES, CONV_HALO, V7X_LANES), jnp.float32),
            pltpu.VMEM((rows, d_model), jnp.float32),
            pltpu.SemaphoreType.DMA(()),
        ],
        compiler_params=pltpu.CompilerParams(
            dimension_semantics=("arbitrary", "arbitrary"),
            vmem_limit_bytes=_FFN_VMEM_BYTES,
        ),
        name="ffn",
    )(h2, x1, w_up, w_up, ffn_conv_w, ffn_conv_w, ffn_conv_b, ffn_conv_b, w_down, g_final)


_MIXER_VMEM_BYTES = 56 * 1024 * 1024
_FFN_VMEM_BYTES = 60 * 1024 * 1024


def _block_diag(pool_w):
    groups, dim, _ = pool_w.shape
    bands = [jnp.pad(pool_w[g], ((0, 0), (g * dim, (groups - 1 - g) * dim))) for g in range(groups)]
    return jnp.concatenate(bands, axis=0)


def kernel(x, mem, g_mix, g_mem, w_in, conv_w, pool_w, pool_scale, w_kv, w_out, g_ffn, w_up,
           ffn_conv_w, ffn_conv_b, w_down, g_final):
    assert x.shape[0] == 1 and mem.shape[0] == 1 and w_in.shape[0] == 1
    assert tuple(sorted(POOL_WINDOWS)) == POOL_WINDOWS and pool_w.shape[1] == len(POOL_WINDOWS)
    bf16 = jnp.bfloat16
    d_conv = conv_w.shape[2]
    d_pool = pool_scale.shape[1]
    k, v = _kv_proj(mem[0], g_mem[0][None, :], w_kv[0])
    x1, h2, w_up16, w_down16 = _mixer(
        x[0], g_mix[0][None, :], w_in[0], conv_w, _block_diag(pool_w[0]).astype(bf16),
        pool_scale[0][None, :], k, v, w_out[0], g_ffn[0][None, :], w_up[0], w_down[0],
        d_conv=d_conv, d_pool=d_pool)
    out = _ffn(h2, x1, w_up16, ffn_conv_w, ffn_conv_b[0][None, :], w_down16, g_final[None, :])
    return out[None]
```

```python
import functools
import math

import jax
import jax.numpy as jnp
from jax import lax
from jax.experimental import pallas as pl
from jax.experimental.pallas import tpu as pltpu

EPS = 1e-6
POOL_WINDOWS = (2, 4, 8, 16)
XATT_HEADS = 4
CONV_WIDTH = 3

V7X_SUBLANES = 8
V7X_LANES = 128
BF16_SUBLANES = 2 * V7X_SUBLANES
V7X_VMEM_BYTES = 64 * 1024 * 1024

MIX_ROWS = 256
MIX_STAGE_ROWS = 128
MIX_STAGE_SLOTS = 4
FFN_ROWS = 1024
FFN_SUB_ROWS = 512
FFN_NORM_ROWS = 128
FFN_COLS = 512
POOL_HALO = 16
CONV_HALO = V7X_SUBLANES


def _rmsnorm(x, g):
    return x * lax.rsqrt(jnp.mean(x * x, axis=-1, keepdims=True) + EPS) * g


def _dot(a, b):
    return jnp.dot(a, b, preferred_element_type=jnp.float32)


def _causal_conv3(buf_ref, w_ref, rows):
    out = w_ref[2:3, :] * buf_ref[CONV_HALO:CONV_HALO + rows, :]
    for j in range(CONV_WIDTH - 1):
        lag = CONV_WIDTH - 1 - j
        out = out + w_ref[j:j + 1, :] * buf_ref[CONV_HALO - lag:CONV_HALO - lag + rows, :]
    return out


def _kv_kernel(mem_ref, g_ref, w_ref, k_ref, v_ref):
    m = _rmsnorm(mem_ref[...], g_ref[...]).astype(jnp.bfloat16)
    kv = _dot(m, w_ref[...].astype(jnp.bfloat16))
    heads, _, head_dim = k_ref.shape
    for hd in range(heads):
        k_ref[hd] = kv[:, hd * head_dim:(hd + 1) * head_dim].astype(jnp.bfloat16)
        v_ref[hd] = kv[:, (heads + hd) * head_dim:(heads + hd + 1) * head_dim].astype(jnp.bfloat16)


def _kv_proj(mem, g_mem, w_kv):
    n_mem, _ = mem.shape
    head_dim = w_kv.shape[1] // (2 * XATT_HEADS)
    out = jax.ShapeDtypeStruct((XATT_HEADS, n_mem, head_dim), jnp.bfloat16)
    return pl.pallas_call(_kv_kernel, out_shape=(out, out), name="kv_proj")(mem, g_mem, w_kv)


def _load_as_bf16(w_hbm, w16, stage, sems):
    slots, chunk, _ = stage.shape
    n_chunks = w_hbm.shape[0] // chunk
    cols = w_hbm.shape[1]
    ahead = slots - 1
    assert n_chunks >= ahead

    def copy(c):
        slot = c % slots
        return pltpu.make_async_copy(w_hbm.at[pl.ds(pl.multiple_of(c * chunk, chunk), chunk), :],
                                     stage.at[slot, :, pl.ds(0, cols)], sems.at[slot])

    for c in range(ahead):
        copy(c).start()

    def step(c, carry):
        @pl.when(c + ahead < n_chunks)
        def _():
            copy(c + ahead).start()

        copy(c).wait()
        w16[pl.ds(pl.multiple_of(c * chunk, chunk), chunk), :] = stage[c % slots, :, 0:cols].astype(jnp.bfloat16)
        return carry

    lax.fori_loop(0, n_chunks, step, None)


def _mixer_kernel(x_ref, gmix_ref, convw_ref, poolw_ref, pscale_ref, k_ref, v_ref, gffn_ref, win_hbm, wout_hbm,
                  wup_ref, wdown_ref, x1_ref, h2_ref, wup16_ref, wdown16_ref,
                  win_ref, wout_ref, stage, stage_sems, zbuf, pbuf, vbuf, mixbuf, *, d_conv, d_pool, head_dim):
    i = pl.program_id(0)
    rows = x_ref.shape[0]

    @pl.when(i == 0)
    def _():
        pbuf[0:CONV_HALO, :] = jnp.zeros((CONV_HALO, d_conv), jnp.float32)
        vbuf[0:POOL_HALO, :] = jnp.zeros((POOL_HALO, d_pool), jnp.float32)
        _load_as_bf16(win_hbm, win_ref, stage, stage_sems)
        _load_as_bf16(wout_hbm, wout_ref, stage, stage_sems)

    x = x_ref[...]
    h = _rmsnorm(x, gmix_ref[...]).astype(jnp.bfloat16)
    zbuf[...] = _dot(h, win_ref[...])

    def cast_up_chunks(lo, hi):
        ff_cols = wup16_ref.shape[2]
        for c in range(lo, hi):
            wup16_ref[c] = wup_ref[:, c * ff_cols:(c + 1) * ff_cols].astype(jnp.bfloat16)

    up_chunks = wup16_ref.shape[0]
    cast_up_chunks(0, up_chunks // 3)

    c0 = d_conv
    pbuf[CONV_HALO:CONV_HALO + rows, :] = zbuf[:, c0:2 * c0] * zbuf[:, 2 * c0:3 * c0]
    conv = _causal_conv3(pbuf, convw_ref, rows)
    pbuf[0:CONV_HALO, :] = pbuf[rows:rows + CONV_HALO, :]
    mixbuf[:, 0:c0] = (zbuf[:, 0:c0] * conv).astype(jnp.bfloat16)

    cast_up_chunks(up_chunks // 3, 2 * up_chunks // 3)

    p0 = 3 * d_conv
    vbuf[POOL_HALO:POOL_HALO + rows, :] = zbuf[:, p0:p0 + d_pool]
    group_dim = d_pool // len(POOL_WINDOWS)
    t = (i * rows + lax.broadcasted_iota(jnp.int32, (rows, 1), 0)).astype(jnp.float32)
    for c in range(0, d_pool, V7X_LANES):
        col = c + lax.broadcasted_iota(jnp.int32, (1, V7X_LANES), 1)
        cur = vbuf[POOL_HALO:POOL_HALO + rows, c:c + V7X_LANES]
        acc = cur
        lag = 1
        pooled = None
        for g in range(c // group_dim, (c + V7X_LANES - 1) // group_dim + 1):
            k = POOL_WINDOWS[g]
            while lag < k:
                acc = acc + vbuf[POOL_HALO - lag:POOL_HALO - lag + rows, c:c + V7X_LANES]
                lag += 1
            mean_k = acc / jnp.minimum(t + 1.0, float(k))
            pooled = mean_k if pooled is None else jnp.where(col >= g * group_dim, mean_k, pooled)
        mixbuf[:, c0 + c:c0 + c + V7X_LANES] = (pooled - cur).astype(jnp.bfloat16)
    vbuf[0:POOL_HALO, :] = vbuf[rows:rows + POOL_HALO, :]
    pool_out = _dot(mixbuf[:, c0:c0 + d_pool], poolw_ref[...]) * pscale_ref[...]
    mixbuf[:, c0:c0 + d_pool] = pool_out.astype(jnp.bfloat16)

    cast_up_chunks(2 * up_chunks // 3, up_chunks)

    q0 = p0 + d_pool
    a0 = c0 + d_pool
    q = jnp.stack([zbuf[:, q0 + hd * head_dim:q0 + (hd + 1) * head_dim] for hd in range(XATT_HEADS)])
    s = jnp.einsum("hqd,hmd->hqm", q.astype(jnp.bfloat16), k_ref[...],
                   preferred_element_type=jnp.float32) * (1.0 / math.sqrt(head_dim))
    e = jnp.exp(s - jnp.max(s, axis=-1, keepdims=True))
    probs = (e / jnp.sum(e, axis=-1, keepdims=True)).astype(jnp.bfloat16)
    att = jnp.einsum("hqm,hmd->hqd", probs, v_ref[...], preferred_element_type=jnp.float32)
    for hd in range(XATT_HEADS):
        mixbuf[:, a0 + hd * head_dim:a0 + (hd + 1) * head_dim] = att[hd].astype(jnp.bfloat16)

    wdown16_ref[...] = wdown_ref[...].astype(jnp.bfloat16)
    x1 = x + _dot(mixbuf[...], wout_ref[...])
    x1_ref[...] = x1
    h2_ref[...] = pltpu.bitcast(_rmsnorm(x1, gffn_ref[...]).astype(jnp.bfloat16), jnp.uint32)


def _const_spec(shape):
    return pl.BlockSpec(shape, lambda i: (0,) * len(shape), pipeline_mode=pl.Buffered(1))


def _mixer(x, g_mix, w_in, conv_w, pool_bd, pool_scale, k, v, w_out, g_ffn, w_up, w_down, *, d_conv, d_pool):
    seq, d_model = x.shape
    rows = MIX_ROWS
    steps = seq // rows
    assert seq % rows == 0 and rows % POOL_HALO == 0
    head_dim = k.shape[2]
    row_spec = pl.BlockSpec((rows, d_model), lambda i: (i, 0))
    consts = (g_mix, conv_w, pool_bd, pool_scale, k, v, g_ffn)
    assert w_in.shape[0] % MIX_STAGE_ROWS == 0 and w_out.shape[0] % MIX_STAGE_ROWS == 0
    assert w_out.shape[1] <= w_in.shape[1]
    d_ff = w_down.shape[0]
    up_chunks = w_up.shape[1] // FFN_COLS
    up_rows = d_model // steps
    down_rows = 2 * d_ff // steps
    assert d_model % steps == 0 and (2 * d_ff) % steps == 0 and steps % 2 == 0
    assert up_rows % BF16_SUBLANES == 0 and down_rows % BF16_SUBLANES == 0 and w_up.shape[1] % FFN_COLS == 0
    return pl.pallas_call(
        functools.partial(_mixer_kernel, d_conv=d_conv, d_pool=d_pool, head_dim=head_dim),
        grid=(steps,),
        in_specs=[row_spec] + [_const_spec(c.shape) for c in consts] + [
            pl.BlockSpec(memory_space=pl.ANY), pl.BlockSpec(memory_space=pl.ANY),
            pl.BlockSpec((up_rows, w_up.shape[1]), lambda i: (i, 0)),
            pl.BlockSpec((down_rows, d_model), lambda i: (i // 2, 0)),
        ],
        out_specs=(row_spec, pl.BlockSpec((rows // 2, d_model), lambda i: (i, 0)),
                   pl.BlockSpec((up_chunks, up_rows, FFN_COLS), lambda i: (0, i, 0)),
                   pl.BlockSpec((down_rows, d_model), lambda i: (i // 2, 0))),
        out_shape=(jax.ShapeDtypeStruct((seq, d_model), jnp.float32),
                   jax.ShapeDtypeStruct((seq // 2, d_model), jnp.uint32),
                   jax.ShapeDtypeStruct((up_chunks, d_model, FFN_COLS), jnp.bfloat16),
                   jax.ShapeDtypeStruct((d_ff, d_model), jnp.bfloat16)),
        scratch_shapes=[
            pltpu.VMEM(w_in.shape, jnp.bfloat16),
            pltpu.VMEM(w_out.shape, jnp.bfloat16),
            pltpu.VMEM((MIX_STAGE_SLOTS, MIX_STAGE_ROWS, w_in.shape[1]), jnp.float32),
            pltpu.SemaphoreType.DMA((MIX_STAGE_SLOTS,)),
            pltpu.VMEM((rows, w_in.shape[1]), jnp.float32),
            pltpu.VMEM((rows + CONV_HALO, d_conv), jnp.float32),
            pltpu.VMEM((rows + POOL_HALO, d_pool), jnp.float32),
            pltpu.VMEM((rows, d_model), jnp.bfloat16),
        ],
        compiler_params=pltpu.CompilerParams(
            dimension_semantics=("arbitrary",),
            vmem_limit_bytes=_MIXER_VMEM_BYTES,
        ),
        name="token_mixer",
    )(x, *consts, w_in, w_out, w_up, w_down)


def _ffn_kernel(h2_ref, x1_hbm, wg_ref, wv_ref, cwg_ref, cwv_ref, bg_ref, bv_ref, wd_ref, gfin_ref,
                out_hbm, gbuf, vbuf, gcarry, vcarry, acc, x1_sems, out_sems):
    i = pl.program_id(0)
    j = pl.program_id(1)
    n_i = pl.num_programs(0)
    last_j = pl.num_programs(1) - 1
    rows = acc.shape[1]
    slot = i % 2
    o_ref = acc.at[slot]

    def x1_copy(tile, s):
        return pltpu.make_async_copy(x1_hbm.at[pl.ds(tile * rows, rows), :], acc.at[s], x1_sems.at[s])

    def out_copy(tile, s):
        return pltpu.make_async_copy(acc.at[s], out_hbm.at[pl.ds(tile * rows, rows), :], out_sems.at[s])

    @pl.when(i == 0)
    def _():
        gcarry[j] = jnp.zeros(gcarry.shape[1:], jnp.float32)
        vcarry[j] = jnp.zeros(vcarry.shape[1:], jnp.float32)

    @pl.when(j == 0)
    def _():
        @pl.when(i == 0)
        def _():
            x1_copy(0, 0).start()

        x1_copy(i, slot).wait()

    @pl.when(j == 1)
    def _():
        @pl.when(i > 0)
        def _():
            out_copy(i - 1, 1 - slot).wait()

        @pl.when(i + 1 < n_i)
        def _():
            x1_copy(i + 1, 1 - slot).start()

    slabs = gbuf.shape[0]
    gbuf[:, 0:CONV_HALO, :] = gcarry[j]
    vbuf[:, 0:CONV_HALO, :] = vcarry[j]
    for w_ref, buf in ((wg_ref, gbuf), (wv_ref, vbuf)):
        for r in range(0, rows, FFN_SUB_ROWS):
            u = _dot(pltpu.bitcast(h2_ref[r // 2:(r + FFN_SUB_ROWS) // 2, :], jnp.bfloat16), w_ref[...])
            for c in range(slabs):
                buf[c, CONV_HALO + r:CONV_HALO + r + FFN_SUB_ROWS, :] = u[:, c * V7X_LANES:(c + 1) * V7X_LANES]
    for r in range(0, rows, FFN_SUB_ROWS):
        part = slice(r, r + FFN_SUB_ROWS)
        halo_part = slice(r, r + CONV_HALO + FFN_SUB_ROWS)
        act = []
        for c in range(slabs):
            lanes = slice(c * V7X_LANES, (c + 1) * V7X_LANES)
            gate = _causal_conv3(gbuf.at[c, halo_part, :], cwg_ref.at[:, lanes], FFN_SUB_ROWS) + bg_ref[:, lanes]
            val = _causal_conv3(vbuf.at[c, halo_part, :], cwv_ref.at[:, lanes], FFN_SUB_ROWS) + bv_ref[:, lanes]
            act.append((gate * jax.nn.sigmoid(gate) * val).astype(jnp.bfloat16))
        o_ref[part, :] += _dot(jnp.concatenate(act, axis=1), wd_ref[...])
    gcarry[j] = gbuf[:, rows:rows + CONV_HALO, :]
    vcarry[j] = vbuf[:, rows:rows + CONV_HALO, :]

    @pl.when(j == last_j)
    def _():
        def norm_rows(c, carry):
            r = pl.multiple_of(c * FFN_NORM_ROWS, FFN_NORM_ROWS)
            o_ref[pl.ds(r, FFN_NORM_ROWS), :] = _rmsnorm(o_ref[pl.ds(r, FFN_NORM_ROWS), :], gfin_ref[...])
            return carry

        lax.fori_loop(0, rows // FFN_NORM_ROWS, norm_rows, None)
        out_copy(i, slot).start()

        @pl.when(i == n_i - 1)
        def _():
            out_copy(i, slot).wait()


def _ffn(h2, x1, w_up, ffn_conv_w, ffn_conv_b, w_down, g_final):
    seq, d_model = x1.shape
    d_ff = w_down.shape[0]
    rows, cols = FFN_ROWS, FFN_COLS
    assert seq % rows == 0 and d_ff % cols == 0 and w_up.shape == (2 * d_ff // cols, d_model, cols)
    assert rows % FFN_SUB_ROWS == 0 and rows % FFN_NORM_ROWS == 0
    n_j = d_ff // cols
    assert n_j >= 2
    gate_cols = lambda r: pl.BlockSpec((r, cols), lambda i, j: (0, j))
    val_cols = lambda r: pl.BlockSpec((r, cols), lambda i, j: (0, j + n_j))
    up_gate = pl.BlockSpec((None, d_model, cols), lambda i, j: (j, 0, 0))
    up_val = pl.BlockSpec((None, d_model, cols), lambda i, j: (j + n_j, 0, 0))
    return pl.pallas_call(
        _ffn_kernel,
        grid=(seq // rows, n_j),
        in_specs=[
            pl.BlockSpec((rows // 2, d_model), lambda i, j: (i, 0)), pl.BlockSpec(memory_space=pl.ANY),
            up_gate, up_val,
            gate_cols(CONV_WIDTH), val_cols(CONV_WIDTH),
            gate_cols(1), val_cols(1),
            pl.BlockSpec((cols, d_model), lambda i, j: (j, 0)),
            pl.BlockSpec((1, d_model), lambda i, j: (0, 0)),
        ],
        out_specs=pl.BlockSpec(memory_space=pl.ANY),
        out_shape=jax.ShapeDtypeStruct((seq, d_model), jnp.float32),
        scratch_shapes=[
            pltpu.VMEM((cols // V7X_LANES, rows + CONV_HALO, V7X_LANES), jnp.float32),
            pltpu.VMEM((cols // V7X_LANES, rows + CONV_HALO, V7X_LANES), jnp.float32),
            pltpu.VMEM((n_j, cols // V7X_LANES, CONV_HALO, V7X_LANES), jnp.float32),
            pltpu.VMEM((n_j, cols // V7X_LANES, CONV_HALO, V7X_LANES), jnp.float32),
            pltpu.VMEM((2, rows, d_model), jnp.float32),
            pltpu.SemaphoreType.DMA((2,)),
            pltpu.SemaphoreType.DMA((2,)),
        ],
        compiler_params=pltpu.CompilerParams(
            dimension_semantics=("arbitrary", "arbitrary"),
            vmem_limit_bytes=_FFN_VMEM_BYTES,
        ),
        name="ffn",
    )(h2, x1, w_up, w_up, ffn_conv_w, ffn_conv_w, ffn_conv_b, ffn_conv_b, w_down, g_final)


_MIXER_VMEM_BYTES = 56 * 1024 * 1024
_FFN_VMEM_BYTES = 60 * 1024 * 1024


def _block_diag(pool_w):
    groups, dim, _ = pool_w.shape
    bands = [jnp.pad(pool_w[g], ((0, 0), (g * dim, (groups - 1 - g) * dim))) for g in range(groups)]
    return jnp.concatenate(bands, axis=0)


def kernel(x, mem, g_mix, g_mem, w_in, conv_w, pool_w, pool_scale, w_kv, w_out, g_ffn, w_up,
           ffn_conv_w, ffn_conv_b, w_down, g_final):
    assert x.shape[0] == 1 and mem.shape[0] == 1 and w_in.shape[0] == 1
    assert tuple(sorted(POOL_WINDOWS)) == POOL_WINDOWS and pool_w.shape[1] == len(POOL_WINDOWS)
    bf16 = jnp.bfloat16
    d_conv = conv_w.shape[2]
    d_pool = pool_scale.shape[1]
    k, v = _kv_proj(mem[0], g_mem[0][None, :], w_kv[0])
    x1, h2, w_up16, w_down16 = _mixer(
        x[0], g_mix[0][None, :], w_in[0], conv_w[0], _block_diag(pool_w[0]).astype(bf16),
        pool_scale[0][None, :], k, v, w_out[0], g_ffn[0][None, :], w_up[0], w_down[0],
        d_conv=d_conv, d_pool=d_pool)
    out = _ffn(h2, x1, w_up16, ffn_conv_w[0], ffn_conv_b[0][None, :], w_down16, g_final[None, :])
    return out[None]
```

```python
import functools
import math

import jax
import jax.numpy as jnp
from jax import lax
from jax.experimental import pallas as pl
from jax.experimental.pallas import tpu as pltpu

EPS = 1e-6
POOL_WINDOWS = (2, 4, 8, 16)
XATT_HEADS = 4
CONV_WIDTH = 3

V7X_SUBLANES = 8
V7X_LANES = 128
BF16_SUBLANES = 2 * V7X_SUBLANES
V7X_VMEM_BYTES = 64 * 1024 * 1024

MIX_ROWS = 256
MIX_STAGE_ROWS = 128
MIX_STAGE_SLOTS = 4
FFN_ROWS = 1024
FFN_SUB_ROWS = 512
FFN_NORM_ROWS = 128
FFN_COLS = 512
POOL_HALO = 16
CONV_HALO = V7X_SUBLANES


def _rmsnorm(x, g):
    return x * lax.rsqrt(jnp.mean(x * x, axis=-1, keepdims=True) + EPS) * g


def _dot(a, b):
    return jnp.dot(a, b, preferred_element_type=jnp.float32)


def _causal_conv3(buf_ref, w_ref, rows):
    out = w_ref[2:3, :] * buf_ref[CONV_HALO:CONV_HALO + rows, :]
    for j in range(CONV_WIDTH - 1):
        lag = CONV_WIDTH - 1 - j
        out = out + w_ref[j:j + 1, :] * buf_ref[CONV_HALO - lag:CONV_HALO - lag + rows, :]
    return out


def _memory_kv(mem_ref, g_ref, wkv_hbm, k_ref, v_ref, m_chunks, kv_acc, stage, sems):
    n_mem = mem_ref.shape[0]
    chunk = stage.shape[1]
    kv_cols = wkv_hbm.shape[1]
    m = _rmsnorm(mem_ref[...], g_ref[...]).astype(jnp.bfloat16)
    for c in range(m_chunks.shape[0]):
        m_chunks[c] = m[:, c * chunk:(c + 1) * chunk]
    kv_acc[0:n_mem, 0:kv_cols] = jnp.zeros((n_mem, kv_cols), jnp.float32)

    def contract(c, w_chunk):
        kv_acc[0:n_mem, 0:kv_cols] += _dot(m_chunks[c], w_chunk)

    _stream_as_bf16(wkv_hbm, stage, sems, contract)
    heads, _, head_dim = k_ref.shape
    for hd in range(heads):
        k_ref[hd] = kv_acc[0:n_mem, hd * head_dim:(hd + 1) * head_dim].astype(jnp.bfloat16)
        v_ref[hd] = kv_acc[0:n_mem, (heads + hd) * head_dim:(heads + hd + 1) * head_dim].astype(jnp.bfloat16)


def _stream_as_bf16(w_hbm, stage, sems, consume):
    slots, chunk, _ = stage.shape
    n_chunks = w_hbm.shape[0] // chunk
    cols = w_hbm.shape[1]
    ahead = slots - 1
    assert n_chunks >= ahead

    def copy(c):
        slot = c % slots
        return pltpu.make_async_copy(w_hbm.at[pl.ds(pl.multiple_of(c * chunk, chunk), chunk), :],
                                     stage.at[slot, :, pl.ds(0, cols)], sems.at[slot])

    for c in range(ahead):
        copy(c).start()

    def step(c, carry):
        @pl.when(c + ahead < n_chunks)
        def _():
            copy(c + ahead).start()

        copy(c).wait()
        consume(c, stage[c % slots, :, 0:cols].astype(jnp.bfloat16))
        return carry

    lax.fori_loop(0, n_chunks, step, None)


def _load_as_bf16(w_hbm, w16, stage, sems):
    chunk = stage.shape[1]

    def store(c, w_chunk):
        w16[pl.ds(pl.multiple_of(c * chunk, chunk), chunk), :] = w_chunk

    _stream_as_bf16(w_hbm, stage, sems, store)


def _mixer_kernel(x_ref, gmix_ref, convw_ref, poolw_ref, pscale_ref, gffn_ref, mem_ref, gmem_ref,
                  win_hbm, wout_hbm, wkv_hbm, wup_ref, wdown_ref, x1_ref, h2_ref, wup16_ref, wdown16_ref,
                  win_ref, wout_ref, m_chunks, k_ref, v_ref, stage, stage_sems, zbuf, pbuf, vbuf, mixbuf,
                  *, d_conv, d_pool, head_dim):
    i = pl.program_id(0)
    rows = x_ref.shape[0]

    @pl.when(i == 0)
    def _():
        pbuf[0:CONV_HALO, :] = jnp.zeros((CONV_HALO, d_conv), jnp.float32)
        vbuf[0:POOL_HALO, :] = jnp.zeros((POOL_HALO, d_pool), jnp.float32)
        _load_as_bf16(win_hbm, win_ref, stage, stage_sems)
        _load_as_bf16(wout_hbm, wout_ref, stage, stage_sems)
        _memory_kv(mem_ref, gmem_ref, wkv_hbm, k_ref, v_ref, m_chunks, zbuf, stage, stage_sems)

    x = x_ref[...]
    h = _rmsnorm(x, gmix_ref[...]).astype(jnp.bfloat16)
    zbuf[...] = _dot(h, win_ref[...])

    def cast_up_chunks(lo, hi):
        ff_cols = wup16_ref.shape[2]
        for c in range(lo, hi):
            wup16_ref[c] = wup_ref[:, c * ff_cols:(c + 1) * ff_cols].astype(jnp.bfloat16)

    up_chunks = wup16_ref.shape[0]
    cast_up_chunks(0, up_chunks // 3)

    c0 = d_conv
    pbuf[CONV_HALO:CONV_HALO + rows, :] = zbuf[:, c0:2 * c0] * zbuf[:, 2 * c0:3 * c0]
    conv = _causal_conv3(pbuf, convw_ref, rows)
    pbuf[0:CONV_HALO, :] = pbuf[rows:rows + CONV_HALO, :]
    mixbuf[:, 0:c0] = (zbuf[:, 0:c0] * conv).astype(jnp.bfloat16)

    cast_up_chunks(up_chunks // 3, 2 * up_chunks // 3)

    p0 = 3 * d_conv
    vbuf[POOL_HALO:POOL_HALO + rows, :] = zbuf[:, p0:p0 + d_pool]
    group_dim = d_pool // len(POOL_WINDOWS)
    t = (i * rows + lax.broadcasted_iota(jnp.int32, (rows, 1), 0)).astype(jnp.float32)
    for c in range(0, d_pool, V7X_LANES):
        col = c + lax.broadcasted_iota(jnp.int32, (1, V7X_LANES), 1)
        cur = vbuf[POOL_HALO:POOL_HALO + rows, c:c + V7X_LANES]
        acc = cur
        lag = 1
        pooled = None
        for g in range(c // group_dim, (c + V7X_LANES - 1) // group_dim + 1):
            k = POOL_WINDOWS[g]
            while lag < k:
                acc = acc + vbuf[POOL_HALO - lag:POOL_HALO - lag + rows, c:c + V7X_LANES]
                lag += 1
            mean_k = acc / jnp.minimum(t + 1.0, float(k))
            pooled = mean_k if pooled is None else jnp.where(col >= g * group_dim, mean_k, pooled)
        mixbuf[:, c0 + c:c0 + c + V7X_LANES] = (pooled - cur).astype(jnp.bfloat16)
    vbuf[0:POOL_HALO, :] = vbuf[rows:rows + POOL_HALO, :]
    pool_out = _dot(mixbuf[:, c0:c0 + d_pool], poolw_ref[...]) * pscale_ref[...]
    mixbuf[:, c0:c0 + d_pool] = pool_out.astype(jnp.bfloat16)

    cast_up_chunks(2 * up_chunks // 3, up_chunks)

    q0 = p0 + d_pool
    a0 = c0 + d_pool
    q = jnp.stack([zbuf[:, q0 + hd * head_dim:q0 + (hd + 1) * head_dim] for hd in range(XATT_HEADS)])
    s = jnp.einsum("hqd,hmd->hqm", q.astype(jnp.bfloat16), k_ref[...],
                   preferred_element_type=jnp.float32) * (1.0 / math.sqrt(head_dim))
    e = jnp.exp(s - jnp.max(s, axis=-1, keepdims=True))
    probs = (e / jnp.sum(e, axis=-1, keepdims=True)).astype(jnp.bfloat16)
    att = jnp.einsum("hqm,hmd->hqd", probs, v_ref[...], preferred_element_type=jnp.float32)
    for hd in range(XATT_HEADS):
        mixbuf[:, a0 + hd * head_dim:a0 + (hd + 1) * head_dim] = att[hd].astype(jnp.bfloat16)

    wdown16_ref[...] = wdown_ref[...].astype(jnp.bfloat16)
    x1 = x + _dot(mixbuf[...], wout_ref[...])
    x1_ref[...] = x1
    h2_ref[...] = pltpu.bitcast(_rmsnorm(x1, gffn_ref[...]).astype(jnp.bfloat16), jnp.uint32)


def _const_spec(shape):
    return pl.BlockSpec(shape, lambda i: (0,) * len(shape), pipeline_mode=pl.Buffered(1))


def _mixer(x, g_mix, w_in, conv_w, pool_bd, pool_scale, mem, g_mem, w_kv, w_out, g_ffn, w_up, w_down,
           *, d_conv, d_pool):
    seq, d_model = x.shape
    rows = MIX_ROWS
    steps = seq // rows
    assert seq % rows == 0 and rows % POOL_HALO == 0
    head_dim = w_kv.shape[1] // (2 * XATT_HEADS)
    row_spec = pl.BlockSpec((rows, d_model), lambda i: (i, 0))
    consts = (g_mix, conv_w, pool_bd, pool_scale, g_ffn, mem, g_mem)
    for w in (w_in, w_out, w_kv):
        assert w.shape[0] % MIX_STAGE_ROWS == 0 and w.shape[1] <= w_in.shape[1]
    assert mem.shape[0] <= rows and mem.shape[1] == w_kv.shape[0]
    d_ff = w_down.shape[0]
    up_chunks = w_up.shape[1] // FFN_COLS
    up_rows = d_model // steps
    down_rows = 2 * d_ff // steps
    assert d_model % steps == 0 and (2 * d_ff) % steps == 0 and steps % 2 == 0
    assert up_rows % BF16_SUBLANES == 0 and down_rows % BF16_SUBLANES == 0 and w_up.shape[1] % FFN_COLS == 0
    return pl.pallas_call(
        functools.partial(_mixer_kernel, d_conv=d_conv, d_pool=d_pool, head_dim=head_dim),
        grid=(steps,),
        in_specs=[row_spec] + [_const_spec(c.shape) for c in consts] + [
            pl.BlockSpec(memory_space=pl.ANY), pl.BlockSpec(memory_space=pl.ANY), pl.BlockSpec(memory_space=pl.ANY),
            pl.BlockSpec((up_rows, w_up.shape[1]), lambda i: (i, 0)),
            pl.BlockSpec((down_rows, d_model), lambda i: (i // 2, 0)),
        ],
        out_specs=(row_spec, pl.BlockSpec((rows // 2, d_model), lambda i: (i, 0)),
                   pl.BlockSpec((up_chunks, up_rows, FFN_COLS), lambda i: (0, i, 0)),
                   pl.BlockSpec((down_rows, d_model), lambda i: (i // 2, 0))),
        out_shape=(jax.ShapeDtypeStruct((seq, d_model), jnp.float32),
                   jax.ShapeDtypeStruct((seq // 2, d_model), jnp.uint32),
                   jax.ShapeDtypeStruct((up_chunks, d_model, FFN_COLS), jnp.bfloat16),
                   jax.ShapeDtypeStruct((d_ff, d_model), jnp.bfloat16)),
        scratch_shapes=[
            pltpu.VMEM(w_in.shape, jnp.bfloat16),
            pltpu.VMEM(w_out.shape, jnp.bfloat16),
            pltpu.VMEM((w_kv.shape[0] // MIX_STAGE_ROWS, mem.shape[0], MIX_STAGE_ROWS), jnp.bfloat16),
            pltpu.VMEM((XATT_HEADS, mem.shape[0], head_dim), jnp.bfloat16),
            pltpu.VMEM((XATT_HEADS, mem.shape[0], head_dim), jnp.bfloat16),
            pltpu.VMEM((MIX_STAGE_SLOTS, MIX_STAGE_ROWS, w_in.shape[1]), jnp.float32),
            pltpu.SemaphoreType.DMA((MIX_STAGE_SLOTS,)),
            pltpu.VMEM((rows, w_in.shape[1]), jnp.float32),
            pltpu.VMEM((rows + CONV_HALO, d_conv), jnp.float32),
            pltpu.VMEM((rows + POOL_HALO, d_pool), jnp.float32),
            pltpu.VMEM((rows, d_model), jnp.bfloat16),
        ],
        compiler_params=pltpu.CompilerParams(
            dimension_semantics=("arbitrary",),
            vmem_limit_bytes=_MIXER_VMEM_BYTES,
        ),
        name="token_mixer",
    )(x, *consts, w_in, w_out, w_kv, w_up, w_down)


def _ffn_kernel(h2_ref, x1_hbm, wg_ref, wv_ref, cwg_ref, cwv_ref, bg_ref, bv_ref, wd_ref, gfin_ref,
                out_hbm, gbuf, vbuf, gcarry, vcarry, acc, x1_sems, out_sems):
    i = pl.program_id(0)
    j = pl.program_id(1)
    n_i = pl.num_programs(0)
    last_j = pl.num_programs(1) - 1
    rows = acc.shape[1]
    slot = i % 2
    o_ref = acc.at[slot]

    def x1_copy(tile, s):
        return pltpu.make_async_copy(x1_hbm.at[pl.ds(tile * rows, rows), :], acc.at[s], x1_sems.at[s])

    def out_copy(tile, s):
        return pltpu.make_async_copy(acc.at[s], out_hbm.at[pl.ds(tile * rows, rows), :], out_sems.at[s])

    @pl.when(i == 0)
    def _():
        gcarry[j] = jnp.zeros(gcarry.shape[1:], jnp.float32)
        vcarry[j] = jnp.zeros(vcarry.shape[1:], jnp.float32)

    @pl.when(j == 0)
    def _():
        @pl.when(i == 0)
        def _():
            x1_copy(0, 0).start()

        x1_copy(i, slot).wait()

    @pl.when(j == 1)
    def _():
        @pl.when(i > 0)
        def _():
            out_copy(i - 1, 1 - slot).wait()

        @pl.when(i + 1 < n_i)
        def _():
            x1_copy(i + 1, 1 - slot).start()

    slabs = gbuf.shape[0]
    gbuf[:, 0:CONV_HALO, :] = gcarry[j]
    vbuf[:, 0:CONV_HALO, :] = vcarry[j]
    for w_ref, buf in ((wg_ref, gbuf), (wv_ref, vbuf)):
        for r in range(0, rows, FFN_SUB_ROWS):
            u = _dot(pltpu.bitcast(h2_ref[r // 2:(r + FFN_SUB_ROWS) // 2, :], jnp.bfloat16), w_ref[...])
            for c in range(slabs):
                buf[c, CONV_HALO + r:CONV_HALO + r + FFN_SUB_ROWS, :] = u[:, c * V7X_LANES:(c + 1) * V7X_LANES]
    for r in range(0, rows, FFN_SUB_ROWS):
        part = slice(r, r + FFN_SUB_ROWS)
        halo_part = slice(r, r + CONV_HALO + FFN_SUB_ROWS)
        act = []
        for c in range(slabs):
            lanes = slice(c * V7X_LANES, (c + 1) * V7X_LANES)
            gate = _causal_conv3(gbuf.at[c, halo_part, :], cwg_ref.at[:, lanes], FFN_SUB_ROWS) + bg_ref[:, lanes]
            val = _causal_conv3(vbuf.at[c, halo_part, :], cwv_ref.at[:, lanes], FFN_SUB_ROWS) + bv_ref[:, lanes]
            act.append((gate * jax.nn.sigmoid(gate) * val).astype(jnp.bfloat16))
        o_ref[part, :] += _dot(jnp.concatenate(act, axis=1), wd_ref[...])
    gcarry[j] = gbuf[:, rows:rows + CONV_HALO, :]
    vcarry[j] = vbuf[:, rows:rows + CONV_HALO, :]

    @pl.when(j == last_j)
    def _():
        def norm_rows(c, carry):
            r = pl.multiple_of(c * FFN_NORM_ROWS, FFN_NORM_ROWS)
            o_ref[pl.ds(r, FFN_NORM_ROWS), :] = _rmsnorm(o_ref[pl.ds(r, FFN_NORM_ROWS), :], gfin_ref[...])
            return carry

        lax.fori_loop(0, rows // FFN_NORM_ROWS, norm_rows, None)
        out_copy(i, slot).start()

        @pl.when(i == n_i - 1)
        def _():
            out_copy(i, slot).wait()


def _ffn(h2, x1, w_up, ffn_conv_w, ffn_conv_b, w_down, g_final):
    seq, d_model = x1.shape
    d_ff = w_down.shape[0]
    rows, cols = FFN_ROWS, FFN_COLS
    assert seq % rows == 0 and d_ff % cols == 0 and w_up.shape == (2 * d_ff // cols, d_model, cols)
    assert rows % FFN_SUB_ROWS == 0 and rows % FFN_NORM_ROWS == 0
    n_j = d_ff // cols
    assert n_j >= 2
    gate_cols = lambda r: pl.BlockSpec((r, cols), lambda i, j: (0, j))
    val_cols = lambda r: pl.BlockSpec((r, cols), lambda i, j: (0, j + n_j))
    up_gate = pl.BlockSpec((None, d_model, cols), lambda i, j: (j, 0, 0))
    up_val = pl.BlockSpec((None, d_model, cols), lambda i, j: (j + n_j, 0, 0))
    return pl.pallas_call(
        _ffn_kernel,
        grid=(seq // rows, n_j),
        in_specs=[
            pl.BlockSpec((rows // 2, d_model), lambda i, j: (i, 0)), pl.BlockSpec(memory_space=pl.ANY),
            up_gate, up_val,
            gate_cols(CONV_WIDTH), val_cols(CONV_WIDTH),
            gate_cols(1), val_cols(1),
            pl.BlockSpec((cols, d_model), lambda i, j: (j, 0)),
            pl.BlockSpec((1, d_model), lambda i, j: (0, 0)),
        ],
        out_specs=pl.BlockSpec(memory_space=pl.ANY),
        out_shape=jax.ShapeDtypeStruct((seq, d_model), jnp.float32),
        scratch_shapes=[
            pltpu.VMEM((cols // V7X_LANES, rows + CONV_HALO, V7X_LANES), jnp.float32),
            pltpu.VMEM((cols // V7X_LANES, rows + CONV_HALO, V7X_LANES), jnp.float32),
            pltpu.VMEM((n_j, cols // V7X_LANES, CONV_HALO, V7X_LANES), jnp.float32),
            pltpu.VMEM((n_j, cols // V7X_LANES, CONV_HALO, V7X_LANES), jnp.float32),
            pltpu.VMEM((2, rows, d_model), jnp.float32),
            pltpu.SemaphoreType.DMA((2,)),
            pltpu.SemaphoreType.DMA((2,)),
        ],
        compiler_params=pltpu.CompilerParams(
            dimension_semantics=("arbitrary", "arbitrary"),
            vmem_limit_bytes=_FFN_VMEM_BYTES,
        ),
        name="ffn",
    )(h2, x1, w_up, w_up, ffn_conv_w, ffn_conv_w, ffn_conv_b, ffn_conv_b, w_down, g_final)


_MIXER_VMEM_BYTES = 60 * 1024 * 1024
_FFN_VMEM_BYTES = 60 * 1024 * 1024


def _block_diag(pool_w):
    groups, dim, _ = pool_w.shape
    bands = [jnp.pad(pool_w[g], ((0, 0), (g * dim, (groups - 1 - g) * dim))) for g in range(groups)]
    return jnp.concatenate(bands, axis=0)


def kernel(x, mem, g_mix, g_mem, w_in, conv_w, pool_w, pool_scale, w_kv, w_out, g_ffn, w_up,
           ffn_conv_w, ffn_conv_b, w_down, g_final):
    assert x.shape[0] == 1 and mem.shape[0] == 1 and w_in.shape[0] == 1
    assert tuple(sorted(POOL_WINDOWS)) == POOL_WINDOWS and pool_w.shape[1] == len(POOL_WINDOWS)
    bf16 = jnp.bfloat16
    d_conv = conv_w.shape[2]
    d_pool = pool_scale.shape[1]
    x1, h2, w_up16, w_down16 = _mixer(
        x[0], g_mix[0][None, :], w_in[0], conv_w[0], _block_diag(pool_w[0]).astype(bf16),
        pool_scale[0][None, :], mem[0], g_mem[0][None, :], w_kv[0], w_out[0], g_ffn[0][None, :], w_up[0],
        w_down[0],
        d_conv=d_conv, d_pool=d_pool)
    out = _ffn(h2, x1, w_up16, ffn_conv_w[0], ffn_conv_b[0][None, :], w_down16, g_final[None, :])
    return out[None]
```

```python
import functools
import math

import jax
import jax.numpy as jnp
from jax import lax
from jax.experimental import pallas as pl
from jax.experimental.pallas import tpu as pltpu

EPS = 1e-6
POOL_WINDOWS = (2, 4, 8, 16)
XATT_HEADS = 4
CONV_WIDTH = 3

V7X_SUBLANES = 8
V7X_LANES = 128
BF16_SUBLANES = 2 * V7X_SUBLANES
V7X_VMEM_BYTES = 64 * 1024 * 1024

MIX_ROWS = 256
MIX_STAGE_ROWS = 128
MIX_STAGE_SLOTS = 4
FFN_ROWS = 1024
FFN_SUB_ROWS = 512
FFN_NORM_ROWS = 128
FFN_COLS = 512
POOL_HALO = 16
CONV_HALO = V7X_SUBLANES


def _rmsnorm(x, g):
    return x * lax.rsqrt(jnp.mean(x * x, axis=-1, keepdims=True) + EPS) * g


def _dot(a, b):
    return jnp.dot(a, b, preferred_element_type=jnp.float32)


def _causal_conv3(buf_ref, w_ref, rows):
    out = w_ref[2:3, :] * buf_ref[CONV_HALO:CONV_HALO + rows, :]
    for j in range(CONV_WIDTH - 1):
        lag = CONV_WIDTH - 1 - j
        out = out + w_ref[j:j + 1, :] * buf_ref[CONV_HALO - lag:CONV_HALO - lag + rows, :]
    return out


def _memory_kv(mem_ref, g_ref, wkv_hbm, k_ref, v_ref, m_chunks, kv_acc, stage, sems):
    n_mem = mem_ref.shape[0]
    chunk = stage.shape[1]
    kv_cols = wkv_hbm.shape[1]
    m = _rmsnorm(mem_ref[...], g_ref[...]).astype(jnp.bfloat16)
    for c in range(m_chunks.shape[0]):
        m_chunks[c] = m[:, c * chunk:(c + 1) * chunk]
    kv_acc[0:n_mem, 0:kv_cols] = jnp.zeros((n_mem, kv_cols), jnp.float32)

    def contract(c, w_chunk):
        kv_acc[0:n_mem, 0:kv_cols] += _dot(m_chunks[c], w_chunk)

    _stream_as_bf16(wkv_hbm, stage, sems, contract)
    heads, _, head_dim = k_ref.shape
    for hd in range(heads):
        k_ref[hd] = kv_acc[0:n_mem, hd * head_dim:(hd + 1) * head_dim].astype(jnp.bfloat16)
        v_ref[hd] = kv_acc[0:n_mem, (heads + hd) * head_dim:(heads + hd + 1) * head_dim].astype(jnp.bfloat16)


def _stream_as_bf16(w_hbm, stage, sems, consume):
    slots, chunk, _ = stage.shape
    n_chunks = w_hbm.shape[0] // chunk
    cols = w_hbm.shape[1]
    ahead = slots - 1
    assert n_chunks >= ahead

    def copy(c):
        slot = c % slots
        return pltpu.make_async_copy(w_hbm.at[pl.ds(pl.multiple_of(c * chunk, chunk), chunk), :],
                                     stage.at[slot, :, pl.ds(0, cols)], sems.at[slot])

    for c in range(ahead):
        copy(c).start()

    def step(c, carry):
        @pl.when(c + ahead < n_chunks)
        def _():
            copy(c + ahead).start()

        copy(c).wait()
        consume(c, stage[c % slots, :, 0:cols].astype(jnp.bfloat16))
        return carry

    lax.fori_loop(0, n_chunks, step, None)


def _load_as_bf16(w_hbm, w16, stage, sems):
    chunk = stage.shape[1]

    def store(c, w_chunk):
        w16[pl.ds(pl.multiple_of(c * chunk, chunk), chunk), :] = w_chunk

    _stream_as_bf16(w_hbm, stage, sems, store)


def _mixer_kernel(x_ref, gmix_ref, convw_ref, poolw_ref, pscale_ref, gffn_ref, mem_ref, gmem_ref,
                  win_hbm, wout_hbm, wkv_hbm, wup_ref, wdown_ref, x1_ref, h2_ref, wup16_ref, wdown16_ref,
                  win_ref, wout_ref, m_chunks, k_ref, v_ref, stage, stage_sems, zbuf, pbuf, vbuf, mixbuf,
                  *, d_conv, d_pool, head_dim):
    i = pl.program_id(0)
    rows = x_ref.shape[0]

    @pl.when(i == 0)
    def _():
        pbuf[0:CONV_HALO, :] = jnp.zeros((CONV_HALO, d_conv), jnp.float32)
        vbuf[0:POOL_HALO, :] = jnp.zeros((POOL_HALO, d_pool), jnp.float32)
        _load_as_bf16(win_hbm, win_ref, stage, stage_sems)
        _load_as_bf16(wout_hbm, wout_ref, stage, stage_sems)
        _memory_kv(mem_ref, gmem_ref, wkv_hbm, k_ref, v_ref, m_chunks, zbuf, stage, stage_sems)

    x = x_ref[...]
    h = _rmsnorm(x, gmix_ref[...]).astype(jnp.bfloat16)
    zbuf[...] = _dot(h, win_ref[...])

    def cast_up_chunks(lo, hi):
        ff_cols = wup16_ref.shape[2]
        for c in range(lo, hi):
            wup16_ref[c] = wup_ref[:, c * ff_cols:(c + 1) * ff_cols].astype(jnp.bfloat16)

    up_chunks = wup16_ref.shape[0]
    cast_up_chunks(0, up_chunks // 3)

    c0 = d_conv
    pbuf[CONV_HALO:CONV_HALO + rows, :] = zbuf[:, c0:2 * c0] * zbuf[:, 2 * c0:3 * c0]
    conv = _causal_conv3(pbuf, convw_ref, rows)
    pbuf[0:CONV_HALO, :] = pbuf[rows:rows + CONV_HALO, :]
    mixbuf[:, 0:c0] = (zbuf[:, 0:c0] * conv).astype(jnp.bfloat16)

    cast_up_chunks(up_chunks // 3, 2 * up_chunks // 3)

    p0 = 3 * d_conv
    vbuf[POOL_HALO:POOL_HALO + rows, :] = zbuf[:, p0:p0 + d_pool]
    group_dim = d_pool // len(POOL_WINDOWS)
    t = (i * rows + lax.broadcasted_iota(jnp.int32, (rows, 1), 0)).astype(jnp.float32)
    for c in range(0, d_pool, V7X_LANES):
        col = c + lax.broadcasted_iota(jnp.int32, (1, V7X_LANES), 1)
        cur = vbuf[POOL_HALO:POOL_HALO + rows, c:c + V7X_LANES]
        acc = cur
        lag = 1
        pooled = None
        for g in range(c // group_dim, (c + V7X_LANES - 1) // group_dim + 1):
            k = POOL_WINDOWS[g]
            while lag < k:
                acc = acc + vbuf[POOL_HALO - lag:POOL_HALO - lag + rows, c:c + V7X_LANES]
                lag += 1
            mean_k = acc / jnp.minimum(t + 1.0, float(k))
            pooled = mean_k if pooled is None else jnp.where(col >= g * group_dim, mean_k, pooled)
        mixbuf[:, c0 + c:c0 + c + V7X_LANES] = (pooled - cur).astype(jnp.bfloat16)
    vbuf[0:POOL_HALO, :] = vbuf[rows:rows + POOL_HALO, :]
    pool_out = _dot(mixbuf[:, c0:c0 + d_pool], poolw_ref[...]) * pscale_ref[...]
    mixbuf[:, c0:c0 + d_pool] = pool_out.astype(jnp.bfloat16)

    cast_up_chunks(2 * up_chunks // 3, up_chunks)

    q0 = p0 + d_pool
    a0 = c0 + d_pool
    q = jnp.stack([zbuf[:, q0 + hd * head_dim:q0 + (hd + 1) * head_dim] for hd in range(XATT_HEADS)])
    s = jnp.einsum("hqd,hmd->hqm", q.astype(jnp.bfloat16), k_ref[...],
                   preferred_element_type=jnp.float32) * (1.0 / math.sqrt(head_dim))
    e = jnp.exp(s - jnp.max(s, axis=-1, keepdims=True))
    probs = (e / jnp.sum(e, axis=-1, keepdims=True)).astype(jnp.bfloat16)
    att = jnp.einsum("hqm,hmd->hqd", probs, v_ref[...], preferred_element_type=jnp.float32)
    for hd in range(XATT_HEADS):
        mixbuf[:, a0 + hd * head_dim:a0 + (hd + 1) * head_dim] = att[hd].astype(jnp.bfloat16)

    wdown16_ref[...] = wdown_ref[...].astype(jnp.bfloat16)
    x1 = x + _dot(mixbuf[...], wout_ref[...])
    x1_ref[...] = x1
    h2_ref[...] = pltpu.bitcast(_rmsnorm(x1, gffn_ref[...]).astype(jnp.bfloat16), jnp.uint32)


def _const_spec(shape):
    return pl.BlockSpec(shape, lambda i: (0,) * len(shape), pipeline_mode=pl.Buffered(1))


def _mixer(x, g_mix, w_in, conv_w, pool_bd, pool_scale, mem, g_mem, w_kv, w_out, g_ffn, w_up, w_down,
           *, d_conv, d_pool):
    seq, d_model = x.shape
    rows = MIX_ROWS
    steps = seq // rows
    assert seq % rows == 0 and rows % POOL_HALO == 0
    head_dim = w_kv.shape[1] // (2 * XATT_HEADS)
    row_spec = pl.BlockSpec((rows, d_model), lambda i: (i, 0))
    consts = (g_mix, conv_w, pool_bd, pool_scale, g_ffn, mem, g_mem)
    for w in (w_in, w_out, w_kv):
        assert w.shape[0] % MIX_STAGE_ROWS == 0 and w.shape[1] <= w_in.shape[1]
    assert mem.shape[0] <= rows and mem.shape[1] == w_kv.shape[0]
    d_ff = w_down.shape[0]
    up_chunks = w_up.shape[1] // FFN_COLS
    up_rows = d_model // steps
    down_rows = 2 * d_ff // steps
    assert d_model % steps == 0 and (2 * d_ff) % steps == 0 and steps % 2 == 0
    assert up_rows % BF16_SUBLANES == 0 and down_rows % BF16_SUBLANES == 0 and w_up.shape[1] % FFN_COLS == 0
    return pl.pallas_call(
        functools.partial(_mixer_kernel, d_conv=d_conv, d_pool=d_pool, head_dim=head_dim),
        grid=(steps,),
        in_specs=[row_spec] + [_const_spec(c.shape) for c in consts] + [
            pl.BlockSpec(memory_space=pl.ANY), pl.BlockSpec(memory_space=pl.ANY), pl.BlockSpec(memory_space=pl.ANY),
            pl.BlockSpec((up_rows, w_up.shape[1]), lambda i: (i, 0)),
            pl.BlockSpec((down_rows, d_model), lambda i: (i // 2, 0)),
        ],
        out_specs=(row_spec, pl.BlockSpec((rows // 2, d_model), lambda i: (i, 0)),
                   pl.BlockSpec((up_chunks, up_rows, FFN_COLS), lambda i: (0, i, 0)),
                   pl.BlockSpec((down_rows, d_model), lambda i: (i // 2, 0))),
        out_shape=(jax.ShapeDtypeStruct((seq, d_model), jnp.float32),
                   jax.ShapeDtypeStruct((seq // 2, d_model), jnp.uint32),
                   jax.ShapeDtypeStruct((up_chunks, d_model, FFN_COLS), jnp.bfloat16),
                   jax.ShapeDtypeStruct((d_ff, d_model), jnp.bfloat16)),
        scratch_shapes=[
            pltpu.VMEM(w_in.shape, jnp.bfloat16),
            pltpu.VMEM(w_out.shape, jnp.bfloat16),
            pltpu.VMEM((w_kv.shape[0] // MIX_STAGE_ROWS, mem.shape[0], MIX_STAGE_ROWS), jnp.bfloat16),
            pltpu.VMEM((XATT_HEADS, mem.shape[0], head_dim), jnp.bfloat16),
            pltpu.VMEM((XATT_HEADS, mem.shape[0], head_dim), jnp.bfloat16),
            pltpu.VMEM((MIX_STAGE_SLOTS, MIX_STAGE_ROWS, w_in.shape[1]), jnp.float32),
            pltpu.SemaphoreType.DMA((MIX_STAGE_SLOTS,)),
            pltpu.VMEM((rows, w_in.shape[1]), jnp.float32),
            pltpu.VMEM((rows + CONV_HALO, d_conv), jnp.float32),
            pltpu.VMEM((rows + POOL_HALO, d_pool), jnp.float32),
            pltpu.VMEM((rows, d_model), jnp.bfloat16),
        ],
        compiler_params=pltpu.CompilerParams(
            dimension_semantics=("arbitrary",),
            vmem_limit_bytes=_MIXER_VMEM_BYTES,
        ),
        name="token_mixer",
    )(x, *consts, w_in, w_out, w_kv, w_up, w_down)


def _ffn_kernel(h2_ref, x1_hbm, wg_ref, wv_ref, cwg_ref, cwv_ref, bg_ref, bv_ref, wd_ref, gfin_ref,
                out_hbm, gbuf, vbuf, gcarry, vcarry, acc, x1_sems, out_sems):
    i = pl.program_id(0)
    j = pl.program_id(1)
    n_i = pl.num_programs(0)
    last_j = pl.num_programs(1) - 1
    rows = acc.shape[1]
    slot = i % 2
    o_ref = acc.at[slot]

    def x1_copy(tile, s):
        return pltpu.make_async_copy(x1_hbm.at[pl.ds(tile * rows, rows), :], acc.at[s], x1_sems.at[s])

    def out_copy(tile, s, c):
        r = pl.multiple_of(c * FFN_NORM_ROWS, FFN_NORM_ROWS)
        return pltpu.make_async_copy(acc.at[s, pl.ds(r, FFN_NORM_ROWS), :],
                                     out_hbm.at[pl.ds(tile * rows + r, FFN_NORM_ROWS), :], out_sems.at[s])

    def wait_out(tile, s):
        def wait_chunk(c, carry):
            out_copy(tile, s, c).wait()
            return carry

        lax.fori_loop(0, rows // FFN_NORM_ROWS, wait_chunk, None)

    @pl.when(i == 0)
    def _():
        gcarry[j] = jnp.zeros(gcarry.shape[1:], jnp.float32)
        vcarry[j] = jnp.zeros(vcarry.shape[1:], jnp.float32)

    @pl.when(j == 0)
    def _():
        @pl.when(i == 0)
        def _():
            x1_copy(0, 0).start()

        x1_copy(i, slot).wait()

    @pl.when(j == 1)
    def _():
        @pl.when(i > 0)
        def _():
            wait_out(i - 1, 1 - slot)

        @pl.when(i + 1 < n_i)
        def _():
            x1_copy(i + 1, 1 - slot).start()

    slabs = gbuf.shape[0]
    gbuf[:, 0:CONV_HALO, :] = gcarry[j]
    vbuf[:, 0:CONV_HALO, :] = vcarry[j]
    for w_ref, buf in ((wg_ref, gbuf), (wv_ref, vbuf)):
        for r in range(0, rows, FFN_SUB_ROWS):
            u = _dot(pltpu.bitcast(h2_ref[r // 2:(r + FFN_SUB_ROWS) // 2, :], jnp.bfloat16), w_ref[...])
            for c in range(slabs):
                buf[c, CONV_HALO + r:CONV_HALO + r + FFN_SUB_ROWS, :] = u[:, c * V7X_LANES:(c + 1) * V7X_LANES]
    for r in range(0, rows, FFN_SUB_ROWS):
        part = slice(r, r + FFN_SUB_ROWS)
        halo_part = slice(r, r + CONV_HALO + FFN_SUB_ROWS)
        act = []
        for c in range(slabs):
            lanes = slice(c * V7X_LANES, (c + 1) * V7X_LANES)
            gate = _causal_conv3(gbuf.at[c, halo_part, :], cwg_ref.at[:, lanes], FFN_SUB_ROWS) + bg_ref[:, lanes]
            val = _causal_conv3(vbuf.at[c, halo_part, :], cwv_ref.at[:, lanes], FFN_SUB_ROWS) + bv_ref[:, lanes]
            act.append((gate * jax.nn.sigmoid(gate) * val).astype(jnp.bfloat16))
        o_ref[part, :] += _dot(jnp.concatenate(act, axis=1), wd_ref[...])
    gcarry[j] = gbuf[:, rows:rows + CONV_HALO, :]
    vcarry[j] = vbuf[:, rows:rows + CONV_HALO, :]

    @pl.when(j == last_j)
    def _():
        def norm_rows(c, carry):
            r = pl.multiple_of(c * FFN_NORM_ROWS, FFN_NORM_ROWS)
            o_ref[pl.ds(r, FFN_NORM_ROWS), :] = _rmsnorm(o_ref[pl.ds(r, FFN_NORM_ROWS), :], gfin_ref[...])
            out_copy(i, slot, c).start()
            return carry

        lax.fori_loop(0, rows // FFN_NORM_ROWS, norm_rows, None)

        @pl.when(i == n_i - 1)
        def _():
            wait_out(i, slot)


def _ffn(h2, x1, w_up, ffn_conv_w, ffn_conv_b, w_down, g_final):
    seq, d_model = x1.shape
    d_ff = w_down.shape[0]
    rows, cols = FFN_ROWS, FFN_COLS
    assert seq % rows == 0 and d_ff % cols == 0 and w_up.shape == (2 * d_ff // cols, d_model, cols)
    assert rows % FFN_SUB_ROWS == 0 and rows % FFN_NORM_ROWS == 0
    n_j = d_ff // cols
    assert n_j >= 2
    gate_cols = lambda r: pl.BlockSpec((r, cols), lambda i, j: (0, j))
    val_cols = lambda r: pl.BlockSpec((r, cols), lambda i, j: (0, j + n_j))
    up_gate = pl.BlockSpec((None, d_model, cols), lambda i, j: (j, 0, 0))
    up_val = pl.BlockSpec((None, d_model, cols), lambda i, j: (j + n_j, 0, 0))
    return pl.pallas_call(
        _ffn_kernel,
        grid=(seq // rows, n_j),
        in_specs=[
            pl.BlockSpec((rows // 2, d_model), lambda i, j: (i, 0)), pl.BlockSpec(memory_space=pl.ANY),
            up_gate, up_val,
            gate_cols(CONV_WIDTH), val_cols(CONV_WIDTH),
            gate_cols(1), val_cols(1),
            pl.BlockSpec((cols, d_model), lambda i, j: (j, 0)),
            pl.BlockSpec((1, d_model), lambda i, j: (0, 0)),
        ],
        out_specs=pl.BlockSpec(memory_space=pl.ANY),
        out_shape=jax.ShapeDtypeStruct((seq, d_model), jnp.float32),
        scratch_shapes=[
            pltpu.VMEM((cols // V7X_LANES, rows + CONV_HALO, V7X_LANES), jnp.float32),
            pltpu.VMEM((cols // V7X_LANES, rows + CONV_HALO, V7X_LANES), jnp.float32),
            pltpu.VMEM((n_j, cols // V7X_LANES, CONV_HALO, V7X_LANES), jnp.float32),
            pltpu.VMEM((n_j, cols // V7X_LANES, CONV_HALO, V7X_LANES), jnp.float32),
            pltpu.VMEM((2, rows, d_model), jnp.float32),
            pltpu.SemaphoreType.DMA((2,)),
            pltpu.SemaphoreType.DMA((2,)),
        ],
        compiler_params=pltpu.CompilerParams(
            dimension_semantics=("arbitrary", "arbitrary"),
            vmem_limit_bytes=_FFN_VMEM_BYTES,
        ),
        name="ffn",
    )(h2, x1, w_up, w_up, ffn_conv_w, ffn_conv_w, ffn_conv_b, ffn_conv_b, w_down, g_final)


_MIXER_VMEM_BYTES = 60 * 1024 * 1024
_FFN_VMEM_BYTES = 60 * 1024 * 1024


def _block_diag(pool_w):
    groups, dim, _ = pool_w.shape
    bands = [jnp.pad(pool_w[g], ((0, 0), (g * dim, (groups - 1 - g) * dim))) for g in range(groups)]
    return jnp.concatenate(bands, axis=0)


def kernel(x, mem, g_mix, g_mem, w_in, conv_w, pool_w, pool_scale, w_kv, w_out, g_ffn, w_up,
           ffn_conv_w, ffn_conv_b, w_down, g_final):
    assert x.shape[0] == 1 and mem.shape[0] == 1 and w_in.shape[0] == 1
    assert tuple(sorted(POOL_WINDOWS)) == POOL_WINDOWS and pool_w.shape[1] == len(POOL_WINDOWS)
    bf16 = jnp.bfloat16
    d_conv = conv_w.shape[2]
    d_pool = pool_scale.shape[1]
    x1, h2, w_up16, w_down16 = _mixer(
        x[0], g_mix[0][None, :], w_in[0], conv_w[0], _block_diag(pool_w[0]).astype(bf16),
        pool_scale[0][None, :], mem[0], g_mem[0][None, :], w_kv[0], w_out[0], g_ffn[0][None, :], w_up[0],
        w_down[0],
        d_conv=d_conv, d_pool=d_pool)
    out = _ffn(h2, x1, w_up16, ffn_conv_w[0], ffn_conv_b[0][None, :], w_down16, g_final[None, :])
    return out[None]
```
